```python
import math
import jax, jax.numpy as jnp
from jax import lax
import numpy as np

D_MODEL = 2048
BATCH = 4
SEQ = 4096
DEPTH = 2

N_MIXERS = 4
GROUP_WIDTH = D_MODEL // N_MIXERS
N_HEADS_GLA = 4
N_HEADS_GDN = 4
N_POOL_GROUPS = 4
N_HEADS_FOX = 4
HEAD_DIM = GROUP_WIDTH // 4
GLA_KEY_DIM = HEAD_DIM // 2
GLA_GATE_RANK = 16
GLA_GATE_TAU = 16.0
GDN_CONV = 4
POOL_WINDOWS = (2, 4, 8, 16)
POOL_GROUP_DIM = GROUP_WIDTH // N_POOL_GROUPS
CHUNK = 64
FOX_BLOCK = 128
D_FF = 256 * int(math.ceil(8 * D_MODEL / 3 / 256))
FFN_CONV = 3
EPS = 1e-6

IN_SPLITS = (
    N_HEADS_GLA * GLA_KEY_DIM,
    N_HEADS_GLA * GLA_KEY_DIM,
    GROUP_WIDTH,
    GROUP_WIDTH,
    GLA_GATE_RANK,
    3 * GROUP_WIDTH,
    GROUP_WIDTH,
    N_HEADS_GDN,
    N_HEADS_GDN,
    GROUP_WIDTH,
    GROUP_WIDTH,
    GROUP_WIDTH,
    GROUP_WIDTH,
    N_HEADS_FOX,
)
IN_WIDTH = sum(IN_SPLITS)
SPLIT_IDX = tuple(sum(IN_SPLITS[:i + 1]) for i in range(len(IN_SPLITS) - 1))

kernel_name = "hybrid_parallel_heads_block"


def rmsnorm(x, w):
    xf = x.astype(jnp.float32)
    y = xf * lax.rsqrt(jnp.mean(xf * xf, axis=-1, keepdims=True) + EPS)
    return (y * w.astype(jnp.float32)).astype(x.dtype)


def l2norm(x):
    return x * lax.rsqrt(jnp.sum(x * x, axis=-1, keepdims=True) + EPS)


def causal_dwconv(x, w):
    width, T = w.shape[0], x.shape[1]
    xp = jnp.pad(x, ((0, 0), (width - 1, 0), (0, 0)))
    return sum(xp[:, j:j + T] * w[j] for j in range(width))


def gla_mixer(q, k, v, g_out, g_lr, w_lr, b_lr, norm_w):
    B, T, _ = q.shape
    H, dk, dv, C = N_HEADS_GLA, GLA_KEY_DIM, HEAD_DIM, CHUNK
    N = T // C
    logg = jax.nn.log_sigmoid(g_lr @ w_lr + b_lr) / GLA_GATE_TAU
    to_c = lambda t, d: jnp.transpose(t.reshape(B, N, C, H, d), (1, 0, 3, 2, 4))
    qc, kc, vc = to_c(q * dk ** -0.5, dk), to_c(k, dk), to_c(v, dv)
    bcum = jnp.cumsum(to_c(logg, dk), axis=3)
    causal = jnp.tril(jnp.ones((C, C), bool))[:, :, None]

    def step(S, inp):
        qi, ki, vi, bi = inp
        diff = bi[:, :, :, None, :] - bi[:, :, None, :, :]
        decay = jnp.exp(jnp.where(causal, diff, -jnp.inf))
        att = jnp.einsum('bhtd,bhsd,bhtsd->bhts', qi, ki, decay)
        o = jnp.einsum('bhts,bhsv->bhtv', att, vi) + jnp.einsum('bhtd,bhdv->bhtv', qi * jnp.exp(bi), S)
        blast = bi[:, :, -1:, :]
        S = S * jnp.exp(blast)[:, :, 0, :, None] + jnp.einsum('bhsd,bhsv->bhdv', ki * jnp.exp(blast - bi), vi)
        return S, o

    S0 = jnp.zeros((B, H, dk, dv), jnp.float32)
    _, o = lax.scan(step, S0, (qc, kc, vc, bcum))
    o = jnp.transpose(o, (1, 0, 3, 2, 4)).reshape(B, T, H, dv)
    o = rmsnorm(o, norm_w) * jax.nn.silu(g_out.reshape(B, T, H, dv))
    return o.reshape(B, T, H * dv)


def gated_deltanet_mixer(qkv, g_out, beta_in, a_in, conv_w, a_log, dt_bias, norm_w):
    B, T, _ = qkv.shape
    H, d, C = N_HEADS_GDN, HEAD_DIM, CHUNK
    N = T // C
    qkv = jax.nn.silu(causal_dwconv(qkv, conv_w))
    q, k, v = jnp.split(qkv, 3, axis=-1)
    q = l2norm(q.reshape(B, T, H, d)) * d ** -0.5
    k = l2norm(k.reshape(B, T, H, d))
    v = v.reshape(B, T, H, d)
    beta = jax.nn.sigmoid(beta_in)
    g = -jnp.exp(a_log) * jax.nn.softplus(a_in + dt_bias)
    to_c = lambda t: jnp.moveaxis(t.reshape(B, N, C, H, -1), 3, 1)
    qc, kc, vc = to_c(q), to_c(k), to_c(v)
    bc = to_c(beta[..., None])
    gcum = jnp.cumsum(to_c(g[..., None])[..., 0], axis=-1)
    incl = jnp.tril(jnp.ones((C, C), bool))
    strict = jnp.tril(jnp.ones((C, C), bool), -1)
    decay = jnp.exp(jnp.where(incl, gcum[..., :, None] - gcum[..., None, :], -jnp.inf))
    kbeta = kc * bc
    lmat = jnp.where(strict, jnp.einsum('bhnid,bhnjd->bhnij', kbeta, kc) * decay, 0.0)
    rhs = jnp.concatenate([vc * bc, kbeta * jnp.exp(gcum)[..., None]], axis=-1)
    sol = lax.linalg.triangular_solve(lmat + jnp.eye(C, dtype=lmat.dtype), rhs,
                                      left_side=True, lower=True, unit_diagonal=True)
    u, w = sol[..., :d], sol[..., d:]
    attn = jnp.einsum('bhnid,bhnjd->bhnij', qc, kc) * decay

    def step(S, inp):
        q_i, k_i, u_i, w_i, a_i, g_i = inp
        v_new = u_i - jnp.einsum('bhck,bhkv->bhcv', w_i, S)
        o = (jnp.einsum('bhck,bhkv->bhcv', q_i * jnp.exp(g_i)[..., None], S)
             + jnp.einsum('bhij,bhjv->bhiv', a_i, v_new))
        g_last = g_i[..., -1:]
        S = S * jnp.exp(g_last)[..., None] + jnp.einsum(
            'bhck,bhcv->bhkv', k_i * jnp.exp(g_last - g_i)[..., None], v_new)
        return S, o

    xs = tuple(jnp.moveaxis(t, 2, 0) for t in (qc, kc, u, w, attn, gcum))
    S0 = jnp.zeros((B, H, d, d), jnp.float32)
    _, o = lax.scan(step, S0, xs)
    o = jnp.moveaxis(jnp.moveaxis(o, 0, 2), 1, 3).reshape(B, T, H, d)
    o = rmsnorm(o, norm_w) * jax.nn.silu(g_out.reshape(B, T, H, d))
    return o.reshape(B, T, H * d)


def pool_mixer(u, w_grp, scale):
    B, T, Cw = u.shape
    cs = jnp.pad(jnp.cumsum(u, axis=1), ((0, 0), (1, 0), (0, 0)))
    pos = jnp.arange(1, T + 1, dtype=jnp.float32)
    outs = []
    for gi, win in enumerate(POOL_WINDOWS):
        sl = slice(gi * POOL_GROUP_DIM, (gi + 1) * POOL_GROUP_DIM)
        csg = cs[:, :, sl]
        lower = jnp.pad(csg[:, :T - win + 1], ((0, 0), (win - 1, 0), (0, 0)))
        mean = (csg[:, 1:] - lower) / jnp.minimum(pos, float(win))[None, :, None]
        outs.append(mean - u[:, :, sl])
    dlt = jnp.stack(outs, axis=2)
    y = jnp.einsum('btgc,gcd->btgd', dlt, w_grp).reshape(B, T, Cw)
    return y * scale


def forgetting_attention(q, k, v, f_in, f_bias):
    B, T, _ = q.shape
    H, d = N_HEADS_FOX, HEAD_DIM
    heads = lambda t: jnp.moveaxis(t.reshape(B, T, H, d), 2, 1)
    q, k, v = heads(q) * d ** -0.5, heads(k), heads(v)
    F = jnp.cumsum(jnp.moveaxis(jax.nn.log_sigmoid(f_in + f_bias), 2, 1), axis=-1)
    outs = []
    for i in range(T // FOX_BLOCK):
        s0, e = i * FOX_BLOCK, (i + 1) * FOX_BLOCK
        logits = (jnp.einsum('bhqd,bhkd->bhqk', q[:, :, s0:e], k[:, :, :e])
                  + F[:, :, s0:e, None] - F[:, :, None, :e])
        mask = jnp.arange(s0, e)[:, None] >= jnp.arange(e)[None, :]
        p = jax.nn.softmax(jnp.where(mask, logits, -jnp.inf), axis=-1)
        outs.append(jnp.einsum('bhqk,bhkd->bhqd', p, v[:, :, :e]))
    o = jnp.concatenate(outs, axis=2)
    return jnp.moveaxis(o, 1, 2).reshape(B, T, H * d)


def conv_ffn(h, w_gate, w_up, conv_w, conv_b, w_down):
    gate = causal_dwconv(h @ w_gate, conv_w) + conv_b
    return (jax.nn.silu(gate) * (h @ w_up)) @ w_down


def setup_inputs(seed: int = 0) -> dict:
    key = jax.random.key(seed)
    ks = iter(jax.random.split(key, 32))
    nrm = lambda shape, s: jax.random.normal(next(ks), shape, jnp.float32) * s
    L, D, F = DEPTH, D_MODEL, D_FF
    x = nrm((BATCH, SEQ, D), 1.0)
    c = nrm((BATCH, D), 1.0)
    w_mod = nrm((L, D, 6 * D), 0.5 * D ** -0.5)
    b_mod = nrm((L, 6 * D), 0.02)
    norm_mix = 1.0 + nrm((L, D), 0.1)
    norm_ffn = 1.0 + nrm((L, D), 0.1)
    w_in = nrm((L, D, IN_WIDTH), D ** -0.5)
    gla_w_lr = nrm((L, GLA_GATE_RANK, N_HEADS_GLA * GLA_KEY_DIM), GLA_GATE_RANK ** -0.5)
    gla_b_lr = nrm((L, N_HEADS_GLA * GLA_KEY_DIM), 0.1)
    gla_norm = 1.0 + nrm((L, HEAD_DIM), 0.1)
    gdn_conv = nrm((L, GDN_CONV, 3 * GROUP_WIDTH), GDN_CONV ** -0.5)
    gdn_a_log = jnp.log(jax.random.uniform(next(ks), (L, N_HEADS_GDN), jnp.float32, 1.0, 16.0))
    dt = jnp.exp(jax.random.uniform(next(ks), (L, N_HEADS_GDN), jnp.float32, math.log(1e-3), math.log(1e-1)))
    gdn_dt_bias = dt + jnp.log(-jnp.expm1(-dt))
    gdn_norm = 1.0 + nrm((L, HEAD_DIM), 0.1)
    pool_w = nrm((L, N_POOL_GROUPS, POOL_GROUP_DIM, POOL_GROUP_DIM), POOL_GROUP_DIM ** -0.5)
    pool_scale = 1.0 + nrm((L, GROUP_WIDTH), 0.1)
    fox_f_bias = 2.0 + nrm((L, N_HEADS_FOX), 0.5)
    w_out = nrm((L, D, D), D ** -0.5)
    ffn_w_gate = nrm((L, D, F), D ** -0.5)
    ffn_w_up = nrm((L, D, F), D ** -0.5)
    ffn_conv_w = nrm((L, FFN_CONV, F), FFN_CONV ** -0.5)
    ffn_conv_b = nrm((L, F), 0.02)
    ffn_w_down = nrm((L, F, D), F ** -0.5)
    norm_final = 1.0 + nrm((D,), 0.1)
    return {"x": x, "c": c, "w_mod": w_mod, "b_mod": b_mod, "norm_mix": norm_mix, "norm_ffn": norm_ffn,
            "w_in": w_in, "gla_w_lr": gla_w_lr, "gla_b_lr": gla_b_lr, "gla_norm": gla_norm,
            "gdn_conv": gdn_conv, "gdn_a_log": gdn_a_log, "gdn_dt_bias": gdn_dt_bias, "gdn_norm": gdn_norm,
            "pool_w": pool_w, "pool_scale": pool_scale, "fox_f_bias": fox_f_bias, "w_out": w_out,
            "ffn_w_gate": ffn_w_gate, "ffn_w_up": ffn_w_up, "ffn_conv_w": ffn_conv_w,
            "ffn_conv_b": ffn_conv_b, "ffn_w_down": ffn_w_down, "norm_final": norm_final}


def reference(x, c, w_mod, b_mod, norm_mix, norm_ffn, w_in, gla_w_lr, gla_b_lr, gla_norm,
              gdn_conv, gdn_a_log, gdn_dt_bias, gdn_norm, pool_w, pool_scale, fox_f_bias, w_out,
              ffn_w_gate, ffn_w_up, ffn_conv_w, ffn_conv_b, ffn_w_down, norm_final):
    B, T, D = x.shape
    f32 = lambda t: t.astype(jnp.float32)
    cond = jax.nn.silu(c)
    for l in range(DEPTH):
        mod = (cond @ w_mod[l] + b_mod[l]).reshape(B, 6, D)
        sh1, sc1, g1, sh2, sc2, g2 = [mod[:, i, None, :] for i in range(6)]

        h = rmsnorm(x, norm_mix[l]) * (1.0 + sc1) + sh1
        proj = f32(h @ w_in[l])
        (gla_q, gla_k, gla_v, gla_g, gla_lr, gdn_qkv, gdn_g, gdn_b, gdn_a,
         pool_u, fox_q, fox_k, fox_v, fox_f) = jnp.split(proj, SPLIT_IDX, axis=-1)
        y_a = gla_mixer(gla_q, gla_k, gla_v, gla_g, gla_lr, f32(gla_w_lr[l]), f32(gla_b_lr[l]), gla_norm[l])
        y_b = gated_deltanet_mixer(gdn_qkv, gdn_g, gdn_b, gdn_a, f32(gdn_conv[l]), f32(gdn_a_log[l]),
                                   f32(gdn_dt_bias[l]), gdn_norm[l])
        y_c = pool_mixer(pool_u, f32(pool_w[l]), f32(pool_scale[l]))
        y_d = forgetting_attention(fox_q, fox_k, fox_v, fox_f, f32(fox_f_bias[l]))
        y = jnp.concatenate([y_a, y_b, y_c, y_d], axis=-1).astype(x.dtype) @ w_out[l]
        x = x + g1 * y

        h = rmsnorm(x, norm_ffn[l]) * (1.0 + sc2) + sh2
        x = x + g2 * conv_ffn(h, ffn_w_gate[l], ffn_w_up[l], ffn_conv_w[l], ffn_conv_b[l], ffn_w_down[l])
    return rmsnorm(x, norm_final)
```

```python
import functools
import math

import numpy as np
import jax
import jax.numpy as jnp
from jax import lax
from jax.experimental import pallas as pl
from jax.experimental.pallas import tpu as pltpu

F32 = jnp.float32
BF16 = jnp.bfloat16
HIGHEST = lax.Precision.HIGHEST

D_MODEL = 2048
GROUP_WIDTH = D_MODEL // 4
N_HEADS = 4
HEAD_DIM = GROUP_WIDTH // 4
GLA_KEY_DIM = HEAD_DIM // 2
GLA_GATE_RANK = 16
GLA_GATE_TAU = 16.0
GDN_CONV = 4
POOL_WINDOWS = (2, 4, 8, 16)
CHUNK = 64
D_FF = 256 * int(math.ceil(8 * D_MODEL / 3 / 256))
FFN_CONV = 3
EPS = 1e-6

IN_SPLITS = (256, 256, 512, 512, 16, 1536, 512, 4, 4, 512, 512, 512, 512, 4)
IN_OFFS = tuple(int(sum(IN_SPLITS[:i])) for i in range(len(IN_SPLITS) + 1))

LANES = 128
P_FOX = 0
P_GLA = 1536
P_GDN = 3072
P_GDN_G = 4608
P_POOL = 5120
P_SMALL = 5632
P_WIDTH = 5760
SM_GLA_LR = 0
SM_GDN_B = 16
SM_GDN_A = 20
SM_FOX_F = 24

VMEM_LIMIT = 56 * 1024 * 1024


def _cparams(n_axes):
    return pltpu.CompilerParams(dimension_semantics=("arbitrary",) * n_axes,
                                vmem_limit_bytes=VMEM_LIMIT)


def _dot(a, b, precision=None):
    return jnp.dot(a, b, preferred_element_type=F32, precision=precision)


def _dot_nt(a, b):
    return lax.dot_general(a, b, (((1,), (1,)), ((), ())), preferred_element_type=F32)


def _dot_sel(sel_bf16, x):
    x1 = x.astype(BF16)
    r1 = x - x1.astype(F32)
    x2 = r1.astype(BF16)
    x3 = (r1 - x2.astype(F32)).astype(BF16)
    return _dot(sel_bf16, x1) + _dot(sel_bf16, x2) + _dot(sel_bf16, x3)


def _log_sigmoid(x):
    return jnp.minimum(x, 0.0) - jnp.log1p(jnp.exp(-jnp.abs(x)))


def _softplus(x):
    return jnp.maximum(x, 0.0) + jnp.log1p(jnp.exp(-jnp.abs(x)))


def _silu(x):
    return x * jax.nn.sigmoid(x)


def _rms_mod(x, norm_w, shift, scale):
    y = x * lax.rsqrt(jnp.mean(x * x, axis=-1, keepdims=True) + EPS) * norm_w
    return y * (1.0 + scale) + shift


def _mod_kernel(c_ref, w_ref, b_ref, o_ref):
    cond = _silu(c_ref[...])
    o_ref[0] = _dot(cond, w_ref[0], precision=HIGHEST) + b_ref[0]


def _modulation(c_pad, w_mod, b_mod):
    L, D, N = w_mod.shape
    tn = 1024
    return pl.pallas_call(
        _mod_kernel,
        grid=(L, N // tn),
        in_specs=[pl.BlockSpec((8, D), lambda l, j: (0, 0)),
                  pl.BlockSpec((1, D, tn), lambda l, j: (l, 0, j)),
                  pl.BlockSpec((1, 1, tn), lambda l, j: (l, 0, j))],
        out_specs=pl.BlockSpec((1, 8, tn), lambda l, j: (l, 0, j)),
        out_shape=jax.ShapeDtypeStruct((L, 8, N), F32),
        compiler_params=_cparams(2),
        name="modulation",
    )(c_pad, w_mod, b_mod.reshape(L, 1, N))


def _inproj_kernel(x_ref, mod_ref, nw_ref, w_ref, o_ref, h_ref):
    @pl.when(pl.program_id(1) == 0)
    def _():
        h = _rms_mod(x_ref[...], nw_ref[...], mod_ref[0, 0:1, :], mod_ref[0, 1:2, :])
        h_ref[...] = h.astype(BF16)

    o_ref[...] = _dot(h_ref[...], w_ref[...])


def _in_projection(x2, mod, norm_w, w_bf16, T):
    M, D = x2.shape
    N = w_bf16.shape[1]
    tm, tn = 512, 640
    tpb = T // tm
    return pl.pallas_call(
        _inproj_kernel,
        grid=(M // tm, N // tn),
        in_specs=[pl.BlockSpec((tm, D), lambda i, j: (i, 0)),
                  pl.BlockSpec((1, 6, D), lambda i, j: (i // tpb, 0, 0)),
                  pl.BlockSpec((1, D), lambda i, j: (0, 0)),
                  pl.BlockSpec((D, tn), lambda i, j: (0, j))],
        out_specs=pl.BlockSpec((tm, tn), lambda i, j: (i, j)),
        out_shape=jax.ShapeDtypeStruct((M, N), F32),
        scratch_shapes=[pltpu.VMEM((tm, D), BF16)],
        compiler_params=_cparams(2),
        name="in_projection",
    )(x2, mod, norm_w.reshape(1, D), w_bf16)


GLA_LEVELS = 7


def _chunk_constants():
    i = np.arange(CHUNK)[:, None]
    j = np.arange(CHUNK)[None, :]
    tri = (i >= j).astype(np.float32)
    strict = (i > j).astype(np.float32)
    sels, masks = [], [(i == j).astype(np.float32)]
    for lv in range(1, GLA_LEVELS):
        m = 1 << (lv - 1)
        ref_row = (np.arange(CHUNK) // (2 * m)) * 2 * m + m
        sels.append((j == ref_row[:, None]).astype(np.float32))
        same = (i // (2 * m)) == (j // (2 * m))
        masks.append((same & ((i % (2 * m)) >= m) & ((j % (2 * m)) < m)).astype(np.float32))
    sel_all = np.concatenate(sels, axis=0)
    return tri, strict, sel_all, np.stack(masks)


def _gla_kernel(p_ref, sm_ref, wlr_ref, blr_ref, nw_ref, tri_ref, sel_ref, msk_ref, o_ref, s_ref):
    tb = p_ref.shape[0]
    dk, dv = GLA_KEY_DIM, HEAD_DIM
    kw = N_HEADS * dk

    @pl.when(pl.program_id(1) == 0)
    def _():
        s_ref[...] = jnp.zeros_like(s_ref)

    def chunk(c, carry):
        rows = pl.ds(pl.multiple_of(c * CHUNK, CHUNK), CHUNK)
        lr = sm_ref[rows, SM_GLA_LR:SM_GLA_LR + GLA_GATE_RANK]
        logits = _dot(lr, wlr_ref[...], precision=HIGHEST) + blr_ref[...]
        logg = _log_sigmoid(logits) * (1.0 / GLA_GATE_TAU)
        b = _dot_sel(tri_ref[...], logg)
        q = p_ref[rows, 0:kw] * dk ** -0.5
        k = p_ref[rows, kw:2 * kw]
        refs = _dot_sel(sel_ref[...], b)
        atts = [None] * N_HEADS
        for lv in range(GLA_LEVELS):
            if lv == 0:
                qt, kt = q, k
            else:
                r = refs[(lv - 1) * CHUNK:lv * CHUNK, :]
                qt = q * jnp.exp(jnp.minimum(b - r, 0.0))
                kt = k * jnp.exp(jnp.minimum(r - b, 0.0))
            qt, kt = qt.astype(BF16), kt.astype(BF16)
            for h in range(N_HEADS):
                hs = slice(h * dk, (h + 1) * dk)
                a = _dot_nt(qt[:, hs], kt[:, hs]) * msk_ref[lv]
                atts[h] = a if atts[h] is None else atts[h] + a
        blast = b[CHUNK - 1:CHUNK, :]
        qe = (q * jnp.exp(b)).astype(BF16)
        kl = (k * jnp.exp(blast - b)).astype(BF16)
        dec = jnp.exp(blast)
        for h in range(N_HEADS):
            hs = slice(h * dk, (h + 1) * dk)
            vs = slice(2 * kw + h * dv, 2 * kw + (h + 1) * dv)
            gs = slice(2 * kw + N_HEADS * dv + h * dv, 2 * kw + N_HEADS * dv + (h + 1) * dv)
            v = p_ref[rows, vs]
            st = s_ref[h]
            o = _dot(atts[h].astype(BF16), v.astype(BF16)) + _dot_nt(qe[:, hs], st.astype(BF16))
            s_ref[h] = st * dec[:, hs] + _dot(jnp.transpose(v).astype(BF16), kl[:, hs])
            on = o * lax.rsqrt(jnp.mean(o * o, axis=-1, keepdims=True) + EPS) * nw_ref[...]
            o_ref[rows, h * dv:(h + 1) * dv] = (on * _silu(p_ref[rows, gs])).astype(BF16)
        return carry

    lax.fori_loop(0, tb // CHUNK, chunk, 0)


def _gla_mixer(proj, w_lr, b_lr, norm_w, consts, B, T):
    tri, _, sel_all, masks = consts
    M = proj.shape[0]
    tb = 512
    nt = T // tb
    kw = N_HEADS * GLA_KEY_DIM
    return pl.pallas_call(
        _gla_kernel,
        grid=(B, nt),
        in_specs=[pl.BlockSpec((tb, 1536), lambda b, t: (b * nt + t, P_GLA // 1536)),
                  pl.BlockSpec((tb, LANES), lambda b, t: (b * nt + t, P_SMALL // LANES)),
                  pl.BlockSpec((GLA_GATE_RANK, kw), lambda b, t: (0, 0)),
                  pl.BlockSpec((1, kw), lambda b, t: (0, 0)),
                  pl.BlockSpec((1, HEAD_DIM), lambda b, t: (0, 0)),
                  pl.BlockSpec((CHUNK, CHUNK), lambda b, t: (0, 0)),
                  pl.BlockSpec(sel_all.shape, lambda b, t: (0, 0)),
                  pl.BlockSpec(masks.shape, lambda b, t: (0, 0, 0))],
        out_specs=pl.BlockSpec((tb, GROUP_WIDTH), lambda b, t: (b * nt + t, 0)),
        out_shape=jax.ShapeDtypeStruct((M, GROUP_WIDTH), BF16),
        scratch_shapes=[pltpu.VMEM((N_HEADS, HEAD_DIM, GLA_KEY_DIM), F32)],
        compiler_params=_cparams(2),
        name="gla_mixer",
    )(proj, proj, w_lr, b_lr.reshape(1, kw), norm_w.reshape(1, HEAD_DIM),
      jnp.asarray(tri, BF16), jnp.asarray(sel_all, BF16), jnp.asarray(masks, F32))


def _gdn_kernel(x_ref, g_ref, sm_ref, cw_ref, alog_ref, dtb_ref, nw_ref, tri_ref, strict_ref,
                o_ref, s_ref, halo_ref):
    tb = x_ref.shape[0]
    d = HEAD_DIM
    hw = N_HEADS * d

    @pl.when(pl.program_id(1) == 0)
    def _():
        s_ref[...] = jnp.zeros_like(s_ref)
        halo_ref[...] = jnp.zeros_like(halo_ref)

    def chunk(c, carry):
        r0 = pl.multiple_of(c * CHUNK, CHUNK)
        rows = pl.ds(r0, CHUNK)
        prev = x_ref[pl.ds(pl.multiple_of(jnp.maximum(r0 - 8, 0), 8), 8), :]
        prev = jnp.where(c == 0, halo_ref[...], prev)
        win = jnp.concatenate([prev, x_ref[rows, :]], axis=0)
        conv = win * cw_ref[GDN_CONV - 1:GDN_CONV, :]
        for s in range(1, GDN_CONV):
            conv = conv + pltpu.roll(win, s, axis=0) * cw_ref[GDN_CONV - 1 - s:GDN_CONV - s, :]
        qkv = _silu(conv[8:, :])

        sm = sm_ref[rows, :]
        beta = jax.nn.sigmoid(sm)
        g = -jnp.exp(alog_ref[...]) * _softplus(sm + dtb_ref[...])
        gcum = _dot_sel(tri_ref[...], g)
        gcum_t = jnp.transpose(gcum)
        tri = tri_ref[...].astype(F32)
        strict = strict_ref[...]
        for h in range(N_HEADS):
            gc = gcum[:, SM_GDN_A + h:SM_GDN_A + h + 1]
            gr = gcum_t[SM_GDN_A + h:SM_GDN_A + h + 1, :]
            decay = jnp.exp(jnp.minimum(gc - gr, 0.0)) * tri
            qh = qkv[:, h * d:(h + 1) * d]
            kh = qkv[:, hw + h * d:hw + (h + 1) * d]
            vh = qkv[:, 2 * hw + h * d:2 * hw + (h + 1) * d]
            qn = qh * lax.rsqrt(jnp.sum(qh * qh, axis=-1, keepdims=True) + EPS) * d ** -0.5
            kn = kh * lax.rsqrt(jnp.sum(kh * kh, axis=-1, keepdims=True) + EPS)
            bcol = beta[:, SM_GDN_B + h:SM_GDN_B + h + 1]
            kb = kn * bcol
            kn16 = kn.astype(BF16)
            x = -(_dot_nt(kb.astype(BF16), kn16) * decay * strict)
            tinv = tri * (1.0 - strict) + x
            for _ in range(5):
                x16 = x.astype(BF16)
                x = _dot(x16, x16)
                tinv = tinv + _dot(x.astype(BF16), tinv.astype(BF16))
            rhs = jnp.concatenate([vh * bcol, kb * jnp.exp(gc)], axis=1)
            sol = _dot(tinv.astype(BF16), rhs.astype(BF16))
            u, w = sol[:, :d], sol[:, d:]
            attn = _dot_nt(qn.astype(BF16), kn16) * decay
            st = s_ref[h]
            st16 = st.astype(BF16)
            v_new = u - _dot(w.astype(BF16), st16)
            v16 = v_new.astype(BF16)
            o = _dot((qn * jnp.exp(gc)).astype(BF16), st16) + _dot(attn.astype(BF16), v16)
            glast = gc[CHUNK - 1:CHUNK, :]
            kdec = kn * jnp.exp(glast - gc)
            s_ref[h] = st * jnp.exp(glast) + _dot(jnp.transpose(kdec).astype(BF16), v16)
            on = o * lax.rsqrt(jnp.mean(o * o, axis=-1, keepdims=True) + EPS) * nw_ref[...]
            o_ref[rows, h * d:(h + 1) * d] = (on * _silu(g_ref[rows, h * d:(h + 1) * d])).astype(BF16)
        return carry

    lax.fori_loop(0, tb // CHUNK, chunk, 0)
    halo_ref[...] = x_ref[tb - 8:tb, :]


def _gdn_mixer(proj, conv_w, a_log, dt_bias, norm_w, consts, B, T):
    tri, strict, _, _ = consts
    M = proj.shape[0]
    tb = 512
    nt = T // tb
    pad = lambda v: jnp.zeros((1, LANES), F32).at[0, SM_GDN_A:SM_GDN_A + N_HEADS].set(v)
    return pl.pallas_call(
        _gdn_kernel,
        grid=(B, nt),
        in_specs=[pl.BlockSpec((tb, 1536), lambda b, t: (b * nt + t, P_GDN // 1536)),
                  pl.BlockSpec((tb, GROUP_WIDTH), lambda b, t: (b * nt + t, P_GDN_G // GROUP_WIDTH)),
                  pl.BlockSpec((tb, LANES), lambda b, t: (b * nt + t, P_SMALL // LANES)),
                  pl.BlockSpec((GDN_CONV, 1536), lambda b, t: (0, 0)),
                  pl.BlockSpec((1, LANES), lambda b, t: (0, 0)),
                  pl.BlockSpec((1, LANES), lambda b, t: (0, 0)),
                  pl.BlockSpec((1, HEAD_DIM), lambda b, t: (0, 0)),
                  pl.BlockSpec((CHUNK, CHUNK), lambda b, t: (0, 0)),
                  pl.BlockSpec((CHUNK, CHUNK), lambda b, t: (0, 0))],
        out_specs=pl.BlockSpec((tb, GROUP_WIDTH), lambda b, t: (b * nt + t, 0)),
        out_shape=jax.ShapeDtypeStruct((M, GROUP_WIDTH), BF16),
        scratch_shapes=[pltpu.VMEM((N_HEADS, HEAD_DIM, HEAD_DIM), F32),
                        pltpu.VMEM((8, 1536), F32)],
        compiler_params=_cparams(2),
        name="gdn_mixer",
    )(proj, proj, proj, conv_w, pad(a_log), pad(dt_bias), norm_w.reshape(1, HEAD_DIM),
      jnp.asarray(tri, BF16), jnp.asarray(strict, F32))


POOL_HALO = 16


def _pool_kernel(u_ref, w_ref, sc_ref, o_ref, halo_ref):
    tb = u_ref.shape[0]
    gd = GROUP_WIDTH // len(POOL_WINDOWS)
    t = pl.program_id(1)

    @pl.when(t == 0)
    def _():
        halo_ref[...] = jnp.zeros_like(halo_ref)

    pos = (t * tb + 1 + lax.broadcasted_iota(jnp.int32, (tb, 1), 0)).astype(F32)
    for gi, win in enumerate(POOL_WINDOWS):
        cols = slice(gi * gd, (gi + 1) * gd)
        u = u_ref[:, cols]
        ssum = jnp.concatenate([halo_ref[:, cols], u], axis=0)
        shift = 1
        while shift < win:
            ssum = ssum + pltpu.roll(ssum, shift, axis=0)
            shift *= 2
        mean = ssum[POOL_HALO:, :] / jnp.minimum(pos, float(win))
        y = _dot((mean - u).astype(BF16), w_ref[gi])
        o_ref[:, cols] = (y * sc_ref[:, cols]).astype(BF16)
    halo_ref[...] = u_ref[tb - POOL_HALO:tb, :]


def _pool_mixer(proj, w_bf16, scale, B, T):
    M = proj.shape[0]
    tb = 512
    nt = T // tb
    gd = GROUP_WIDTH // len(POOL_WINDOWS)
    return pl.pallas_call(
        _pool_kernel,
        grid=(B, nt),
        in_specs=[pl.BlockSpec((tb, GROUP_WIDTH), lambda b, t: (b * nt + t, P_POOL // GROUP_WIDTH)),
                  pl.BlockSpec((len(POOL_WINDOWS), gd, gd), lambda b, t: (0, 0, 0)),
                  pl.BlockSpec((1, GROUP_WIDTH), lambda b, t: (0, 0))],
        out_specs=pl.BlockSpec((tb, GROUP_WIDTH), lambda b, t: (b * nt + t, 0)),
        out_shape=jax.ShapeDtypeStruct((M, GROUP_WIDTH), BF16),
        scratch_shapes=[pltpu.VMEM((POOL_HALO, GROUP_WIDTH), F32)],
        compiler_params=_cparams(2),
        name="pool_mixer",
    )(proj, w_bf16, scale.reshape(1, GROUP_WIDTH))


FOX_CUM_BLOCK = 256


def _fox_gate_kernel(sm_ref, bias_ref, tri_ref, o_ref, carry_ref):
    @pl.when(pl.program_id(1) == 0)
    def _():
        carry_ref[...] = jnp.zeros_like(carry_ref)

    lf = _log_sigmoid(sm_ref[...] + bias_ref[...])
    cum = _dot_sel(tri_ref[...], lf) + carry_ref[...]
    carry_ref[...] = cum[FOX_CUM_BLOCK - 1:FOX_CUM_BLOCK, :]
    o_ref[0] = jnp.transpose(cum)[SM_FOX_F:SM_FOX_F + 8, :]


def _fox_gate_cumsum(proj, f_bias, B, T):
    tb = FOX_CUM_BLOCK
    nt = T // tb
    tri = jnp.asarray(np.tril(np.ones((tb, tb), np.float32)), BF16)
    bias = jnp.zeros((1, LANES), F32).at[0, SM_FOX_F:SM_FOX_F + N_HEADS].set(f_bias)
    return pl.pallas_call(
        _fox_gate_kernel,
        grid=(B, nt),
        in_specs=[pl.BlockSpec((tb, LANES), lambda b, t: (b * nt + t, P_SMALL // LANES)),
                  pl.BlockSpec((1, LANES), lambda b, t: (0, 0)),
                  pl.BlockSpec((tb, tb), lambda b, t: (0, 0))],
        out_specs=pl.BlockSpec((1, 8, tb), lambda b, t: (b, 0, t)),
        out_shape=jax.ShapeDtypeStruct((B, 8, T), F32),
        scratch_shapes=[pltpu.VMEM((1, LANES), F32)],
        compiler_params=_cparams(2),
        name="fox_gate_cumsum",
    )(proj, bias, tri)


FOX_BLOCK = 512


def _fox_kernel(q_ref, k_ref, v_ref, f_ref, o_ref, kb_ref, vb_ref):
    qi = pl.program_id(2)
    blk = FOX_BLOCK
    d = HEAD_DIM

    @pl.when(qi == 0)
    def _():
        kb_ref[...] = k_ref[...].astype(BF16)
        vb_ref[...] = v_ref[...].astype(BF16)

    q = (q_ref[...] * d ** -0.5).astype(BF16)
    q0 = pl.multiple_of(qi * blk, blk)
    f0 = f_ref[0, :, pl.ds(q0, LANES)][:, 0:1]

    def block(ki, carry, masked):
        m, l, acc = carry
        k0 = pl.multiple_of(ki * blk, blk)
        fk = f_ref[0, :, pl.ds(k0, blk)] - f0
        s = _dot_nt(q, kb_ref[pl.ds(k0, blk), :]) - fk
        if masked:
            row = lax.broadcasted_iota(jnp.int32, (blk, blk), 0)
            col = lax.broadcasted_iota(jnp.int32, (blk, blk), 1)
            s = jnp.where(row >= col, s, -jnp.inf)
        m_new = jnp.maximum(m, jnp.max(s, axis=-1, keepdims=True))
        alpha = jnp.exp(m - m_new)
        p = jnp.exp(s - m_new)
        l = alpha * l + jnp.sum(p, axis=-1, keepdims=True)
        acc = alpha * acc + _dot(p.astype(BF16), vb_ref[pl.ds(k0, blk), :])
        return m_new, l, acc

    init = (jnp.full((blk, 1), -jnp.inf, F32), jnp.zeros((blk, 1), F32), jnp.zeros((blk, d), F32))
    carry = lax.fori_loop(0, qi, lambda ki, c: block(ki, c, False), init)
    _, l, acc = block(qi, carry, True)
    o_ref[...] = (acc / l).astype(BF16)


def _fox_attention(proj, fcum, B, T):
    M = proj.shape[0]
    blk = FOX_BLOCK
    nq = T // blk
    d = HEAD_DIM
    H = N_HEADS
    return pl.pallas_call(
        _fox_kernel,
        grid=(B, H, nq),
        in_specs=[pl.BlockSpec((blk, d), lambda b, h, i: (b * nq + i, P_FOX // d + h)),
                  pl.BlockSpec((T, d), lambda b, h, i: (b, P_FOX // d + H + h)),
                  pl.BlockSpec((T, d), lambda b, h, i: (b, P_FOX // d + 2 * H + h)),
                  pl.BlockSpec((1, 1, T), lambda b, h, i: (b * 8 + h, 0, 0))],
        out_specs=pl.BlockSpec((blk, d), lambda b, h, i: (b * nq + i, h)),
        out_shape=jax.ShapeDtypeStruct((M, GROUP_WIDTH), BF16),
        scratch_shapes=[pltpu.VMEM((T, d), BF16), pltpu.VMEM((T, d), BF16)],
        compiler_params=_cparams(3),
        name="fox_attention",
    )(proj, proj, proj, fcum.reshape(B * 8, 1, T))


def _outproj_kernel(ya_ref, yb_ref, yc_ref, yd_ref, w_ref, x_ref, mod_ref, o_ref):
    gw = GROUP_WIDTH
    acc = _dot(ya_ref[...], w_ref[0:gw, :])
    acc = acc + _dot(yb_ref[...], w_ref[gw:2 * gw, :])
    acc = acc + _dot(yc_ref[...], w_ref[2 * gw:3 * gw, :])
    acc = acc + _dot(yd_ref[...], w_ref[3 * gw:4 * gw, :])
    o_ref[...] = x_ref[...] + mod_ref[0, 2:3, :] * acc


def _out_projection(ys, w_bf16, x2, mod, T):
    M, D = x2.shape
    tm, tn = 512, 1024
    tpb = T // tm
    yspec = pl.BlockSpec((tm, GROUP_WIDTH), lambda i, j: (i, 0))
    return pl.pallas_call(
        _outproj_kernel,
        grid=(M // tm, D // tn),
        in_specs=[yspec, yspec, yspec, yspec,
                  pl.BlockSpec((D, tn), lambda i, j: (0, j)),
                  pl.BlockSpec((tm, tn), lambda i, j: (i, j)),
                  pl.BlockSpec((1, 6, tn), lambda i, j: (i // tpb, 0, j))],
        out_specs=pl.BlockSpec((tm, tn), lambda i, j: (i, j)),
        out_shape=jax.ShapeDtypeStruct((M, D), F32),
        compiler_params=_cparams(2),
        name="out_projection",
    )(*ys, w_bf16, x2, mod)


def _ffn_kernel(x_ref, mod_ref, nw_ref, wg_ref, wu_ref, cw_ref, cb_ref, wd_ref, o_ref,
                h_ref, acc_ref, halo_ref, *, tiles_per_seq):
    i = pl.program_id(0)
    j = pl.program_id(1)
    tm, tf = x_ref.shape[0], wg_ref.shape[1]

    @pl.when(j == 0)
    def _():
        h = _rms_mod(x_ref[...], nw_ref[...], mod_ref[0, 3:4, :], mod_ref[0, 4:5, :])
        h_ref[...] = h.astype(BF16)
        acc_ref[...] = jnp.zeros_like(acc_ref)

    h = h_ref[...]
    gate = _dot(h, wg_ref[...])
    up = _dot(h, wu_ref[...])
    prev = jnp.where(i % tiles_per_seq == 0, 0.0, halo_ref[j])
    row = lax.broadcasted_iota(jnp.int32, (tm, tf), 0)
    g1 = jnp.where(row == 0, prev[7:8, :], pltpu.roll(gate, 1, axis=0))
    g2 = jnp.where(row == 0, prev[6:7, :], jnp.where(row == 1, prev[7:8, :], pltpu.roll(gate, 2, axis=0)))
    halo_ref[j] = gate[tm - 8:tm, :]
    conv = cw_ref[0:1, :] * g2 + cw_ref[1:2, :] * g1 + cw_ref[2:3, :] * gate + cb_ref[...]
    act = (_silu(conv) * up).astype(BF16)
    acc_ref[...] += _dot(act, wd_ref[...])

    @pl.when(j == pl.num_programs(1) - 1)
    def _():
        o_ref[...] = x_ref[...] + mod_ref[0, 5:6, :] * acc_ref[...]


def _conv_ffn(x2, mod, norm_w, wg, wu, conv_w, conv_b, wd, T):
    M, D = x2.shape
    F = wg.shape[1]
    tm, tf = 512, 512
    tpb = T // tm
    return pl.pallas_call(
        functools.partial(_ffn_kernel, tiles_per_seq=tpb),
        grid=(M // tm, F // tf),
        in_specs=[pl.BlockSpec((tm, D), lambda i, j: (i, 0)),
                  pl.BlockSpec((1, 6, D), lambda i, j: (i // tpb, 0, 0)),
                  pl.BlockSpec((1, D), lambda i, j: (0, 0)),
                  pl.BlockSpec((D, tf), lambda i, j: (0, j)),
                  pl.BlockSpec((D, tf), lambda i, j: (0, j)),
                  pl.BlockSpec((FFN_CONV, tf), lambda i, j: (0, j)),
                  pl.BlockSpec((1, tf), lambda i, j: (0, j)),
                  pl.BlockSpec((tf, D), lambda i, j: (j, 0))],
        out_specs=pl.BlockSpec((tm, D), lambda i, j: (i, 0)),
        out_shape=jax.ShapeDtypeStruct((M, D), F32),
        scratch_shapes=[pltpu.VMEM((tm, D), BF16),
                        pltpu.VMEM((tm, D), F32),
                        pltpu.VMEM((F // tf, 8, tf), F32)],
        compiler_params=_cparams(2),
        name="conv_ffn",
    )(x2, mod, norm_w.reshape(1, D), wg, wu, conv_w, conv_b.reshape(1, F), wd)


def _final_norm_kernel(x_ref, w_ref, o_ref):
    x = x_ref[...]
    o_ref[...] = x * lax.rsqrt(jnp.mean(x * x, axis=-1, keepdims=True) + EPS) * w_ref[...]


def _final_norm(x2, w):
    M, D = x2.shape
    tm = 512
    return pl.pallas_call(
        _final_norm_kernel,
        grid=(M // tm,),
        in_specs=[pl.BlockSpec((tm, D), lambda i: (i, 0)), pl.BlockSpec((1, D), lambda i: (0, 0))],
        out_specs=pl.BlockSpec((tm, D), lambda i: (i, 0)),
        out_shape=jax.ShapeDtypeStruct((M, D), F32),
        compiler_params=_cparams(1),
        name="final_norm",
    )(x2, w.reshape(1, D))


def _reorder_columns(w_in):
    seg = lambda i: w_in[:, IN_OFFS[i]:IN_OFFS[i + 1]]
    small = jnp.concatenate([seg(4), seg(7), seg(8), seg(13)], axis=1)
    small = jnp.pad(small, ((0, 0), (0, LANES - small.shape[1])))
    cols = [seg(10), seg(11), seg(12),
            seg(0), seg(1), seg(2), seg(3),
            seg(5), seg(6),
            seg(9),
            small]
    return jnp.concatenate(cols, axis=1)


def kernel(x, c, w_mod, b_mod, norm_mix, norm_ffn, w_in, gla_w_lr, gla_b_lr, gla_norm, gdn_conv, gdn_a_log, gdn_dt_bias, gdn_norm, pool_w, pool_scale, fox_f_bias, w_out, ffn_w_gate, ffn_w_up, ffn_conv_w, ffn_conv_b, ffn_w_down, norm_final):
    B, T, D = x.shape
    L = w_mod.shape[0]
    assert D == D_MODEL and T % 512 == 0 and B <= 8
    consts = _chunk_constants()

    c_pad = jnp.zeros((8, D), F32).at[:B].set(c)
    mod_all = _modulation(c_pad, w_mod, b_mod)
    x2 = x.reshape(B * T, D)
    for l in range(L):
        mod = mod_all[l, :B].reshape(B, 6, D)
        proj = _in_projection(x2, mod, norm_mix[l], _reorder_columns(w_in[l]).astype(BF16), T)
        y_a = _gla_mixer(proj, gla_w_lr[l], gla_b_lr[l], gla_norm[l], consts, B, T)
        y_b = _gdn_mixer(proj, gdn_conv[l], gdn_a_log[l], gdn_dt_bias[l], gdn_norm[l], consts, B, T)
        y_c = _pool_mixer(proj, pool_w[l].astype(BF16), pool_scale[l], B, T)
        y_d = _fox_attention(proj, _fox_gate_cumsum(proj, fox_f_bias[l], B, T), B, T)
        x2 = _out_projection((y_a, y_b, y_c, y_d), w_out[l].astype(BF16), x2, mod, T)
        x2 = _conv_ffn(x2, mod, norm_ffn[l], ffn_w_gate[l].astype(BF16), ffn_w_up[l].astype(BF16),
                       ffn_conv_w[l], ffn_conv_b[l], ffn_w_down[l].astype(BF16), T)
    return _final_norm(x2, norm_final).reshape(B, T, D)
```

```python
import functools
import math

import numpy as np
import jax
import jax.numpy as jnp
from jax import lax
from jax.experimental import pallas as pl
from jax.experimental.pallas import tpu as pltpu

F32 = jnp.float32
BF16 = jnp.bfloat16
HIGHEST = lax.Precision.HIGHEST

D_MODEL = 2048
GROUP_WIDTH = D_MODEL // 4
N_HEADS = 4
HEAD_DIM = GROUP_WIDTH // 4
GLA_KEY_DIM = HEAD_DIM // 2
GLA_GATE_RANK = 16
GLA_GATE_TAU = 16.0
GDN_CONV = 4
POOL_WINDOWS = (2, 4, 8, 16)
CHUNK = 64
D_FF = 256 * int(math.ceil(8 * D_MODEL / 3 / 256))
FFN_CONV = 3
EPS = 1e-6

IN_SPLITS = (256, 256, 512, 512, 16, 1536, 512, 4, 4, 512, 512, 512, 512, 4)
IN_OFFS = tuple(int(sum(IN_SPLITS[:i])) for i in range(len(IN_SPLITS) + 1))

LANES = 128
P_FOX = 0
P_GLA = 1536
P_GDN = 3072
P_GDN_G = 4608
P_POOL = 5120
P_WIDTH = 5632
SM_GLA_LR = 0
SM_GDN_B = 16
SM_GDN_A = 20
SM_FOX_F = 24

VMEM_LIMIT = 56 * 1024 * 1024
FFN_VMEM_LIMIT = 60 * 1024 * 1024


def _cparams(n_axes):
    return pltpu.CompilerParams(dimension_semantics=("arbitrary",) * n_axes,
                                vmem_limit_bytes=VMEM_LIMIT)


def _dot(a, b, precision=None):
    return jnp.dot(a, b, preferred_element_type=F32, precision=precision)


def _dot_nt(a, b):
    return lax.dot_general(a, b, (((1,), (1,)), ((), ())), preferred_element_type=F32)


def _bdot(a, b):
    return lax.dot_general(a, b, (((2,), (1,)), ((0,), (0,))), preferred_element_type=F32)


def _bdot_nt(a, b):
    return lax.dot_general(a, b, (((2,), (2,)), ((0,), (0,))), preferred_element_type=F32)


def _dot_sel(sel_bf16, x):
    x1 = x.astype(BF16)
    r1 = x - x1.astype(F32)
    x2 = r1.astype(BF16)
    x3 = (r1 - x2.astype(F32)).astype(BF16)
    return _dot(sel_bf16, x1) + _dot(sel_bf16, x2) + _dot(sel_bf16, x3)


def _log_sigmoid(x):
    return jnp.minimum(x, 0.0) - jnp.log1p(jnp.exp(-jnp.abs(x)))


def _softplus(x):
    return jnp.maximum(x, 0.0) + jnp.log1p(jnp.exp(-jnp.abs(x)))


def _silu(x):
    return x * jax.nn.sigmoid(x)


def _rms_mod(x, norm_w, shift, scale):
    y = x * lax.rsqrt(jnp.mean(x * x, axis=-1, keepdims=True) + EPS) * norm_w
    return y * (1.0 + scale) + shift


def _mod_kernel(c_ref, w_ref, b_ref, o_ref):
    cond = _silu(c_ref[...])
    o_ref[0] = _dot(cond, w_ref[0], precision=HIGHEST) + b_ref[0]


def _modulation(c_pad, w_mod, b_mod):
    L, D, N = w_mod.shape
    tn = 1024
    return pl.pallas_call(
        _mod_kernel,
        grid=(L, N // tn),
        in_specs=[pl.BlockSpec((8, D), lambda l, j: (0, 0)),
                  pl.BlockSpec((1, D, tn), lambda l, j: (l, 0, j)),
                  pl.BlockSpec((1, 1, tn), lambda l, j: (l, 0, j))],
        out_specs=pl.BlockSpec((1, 8, tn), lambda l, j: (l, 0, j)),
        out_shape=jax.ShapeDtypeStruct((L, 8, N), F32),
        compiler_params=_cparams(2),
        name="modulation",
    )(c_pad, w_mod, b_mod.reshape(L, 1, N))


def _wprep_kernel(w_ref, main_ref, small_ref):
    seg = lambda i: w_ref[0, :, IN_OFFS[i]:IN_OFFS[i + 1]].astype(BF16)
    off = 0
    for i in (10, 11, 12, 0, 1, 2, 3, 5, 6, 9):
        width = IN_SPLITS[i]
        main_ref[0, :, off:off + width] = seg(i)
        off += width
    small_ref[...] = jnp.zeros_like(small_ref)
    for i, lane in ((4, SM_GLA_LR), (7, SM_GDN_B), (8, SM_GDN_A), (13, SM_FOX_F)):
        small_ref[0, :, lane:lane + IN_SPLITS[i]] = seg(i)


def _prepare_in_weights(w_in):
    L, D, N = w_in.shape
    tr = 256
    return pl.pallas_call(
        _wprep_kernel,
        grid=(L, D // tr),
        in_specs=[pl.BlockSpec((1, tr, N), lambda l, r: (l, r, 0))],
        out_specs=[pl.BlockSpec((1, tr, P_WIDTH), lambda l, r: (l, r, 0)),
                   pl.BlockSpec((1, tr, LANES), lambda l, r: (l, r, 0))],
        out_shape=[jax.ShapeDtypeStruct((L, D, P_WIDTH), BF16),
                   jax.ShapeDtypeStruct((L, D, LANES), BF16)],
        compiler_params=_cparams(2),
        name="in_weight_prep",
    )(w_in)


def _inproj_kernel(x_ref, mod_ref, nw_ref, w_ref, ws_ref, o_ref, os_ref, h_ref):
    @pl.when(pl.program_id(1) == 0)
    def _():
        h = _rms_mod(x_ref[...], nw_ref[...], mod_ref[0, 0:1, :], mod_ref[0, 1:2, :])
        h_ref[...] = h.astype(BF16)
        os_ref[...] = _dot(h_ref[...], ws_ref[0])

    o_ref[...] = _dot(h_ref[...], w_ref[0])


def _in_projection(x2, mod, norm_w, w_main, w_small, l, T):
    M, D = x2.shape
    N = w_main.shape[2]
    tm, tn = 1024, 512
    tpb = T // tm
    return pl.pallas_call(
        _inproj_kernel,
        grid=(M // tm, N // tn),
        in_specs=[pl.BlockSpec((tm, D), lambda i, j: (i, 0)),
                  pl.BlockSpec((1, 6, D), lambda i, j: (i // tpb, 0, 0)),
                  pl.BlockSpec((1, D), lambda i, j: (0, 0)),
                  pl.BlockSpec((1, D, tn), lambda i, j: (l, 0, j)),
                  pl.BlockSpec((1, D, LANES), lambda i, j: (l, 0, 0))],
        out_specs=[pl.BlockSpec((tm, tn), lambda i, j: (i, j)),
                   pl.BlockSpec((tm, LANES), lambda i, j: (i, 0))],
        out_shape=[jax.ShapeDtypeStruct((M, N), F32),
                   jax.ShapeDtypeStruct((M, LANES), F32)],
        scratch_shapes=[pltpu.VMEM((tm, D), BF16)],
        compiler_params=_cparams(2),
        name="in_projection",
    )(x2, mod, norm_w.reshape(1, D), w_main, w_small)


GLA_LEVELS = 7
GLA_GROUP = 2
GDN_GROUP = 2


def _chunk_constants():
    i = np.arange(CHUNK)[:, None]
    j = np.arange(CHUNK)[None, :]
    tri = (i >= j).astype(np.float32)
    strict = (i > j).astype(np.float32)
    sels, masks = [], [(i == j).astype(np.float32)]
    for lv in range(1, GLA_LEVELS):
        m = 1 << (lv - 1)
        ref_row = (np.arange(CHUNK) // (2 * m)) * 2 * m + m
        sels.append((j == ref_row[:, None]).astype(np.float32))
        same = (i // (2 * m)) == (j // (2 * m))
        masks.append((same & ((i % (2 * m)) >= m) & ((j % (2 * m)) < m)).astype(np.float32))
    sel_all = np.concatenate(sels, axis=0)
    return tri, strict, sel_all, np.stack(masks)


def _gla_kernel(p_ref, sm_ref, wlr_ref, blr_ref, nw_ref, tri_ref, sel_ref, msk_ref, o_ref, s_ref):
    tb = p_ref.shape[0]
    dk, dv = GLA_KEY_DIM, HEAD_DIM
    kw = N_HEADS * dk

    @pl.when(pl.program_id(1) == 0)
    def _():
        s_ref[...] = jnp.zeros_like(s_ref)

    def chunk(c):
        rows = pl.ds(pl.multiple_of(c * CHUNK, CHUNK), CHUNK)
        lr = sm_ref[rows, SM_GLA_LR:SM_GLA_LR + GLA_GATE_RANK]
        logits = _dot(lr, wlr_ref[...], precision=HIGHEST) + blr_ref[...]
        logg = _log_sigmoid(logits) * (1.0 / GLA_GATE_TAU)
        b = _dot_sel(tri_ref[...], logg)
        q = p_ref[rows, 0:kw] * dk ** -0.5
        k = p_ref[rows, kw:2 * kw]
        refs = _dot_sel(sel_ref[...], b)
        atts = [None] * N_HEADS
        for lv in range(GLA_LEVELS):
            if lv == 0:
                qt, kt = q, k
            else:
                r = refs[(lv - 1) * CHUNK:lv * CHUNK, :]
                qt = q * jnp.exp(jnp.minimum(b - r, 0.0))
                kt = k * jnp.exp(jnp.minimum(r - b, 0.0))
            qt, kt = qt.astype(BF16), kt.astype(BF16)
            for h in range(N_HEADS):
                hs = slice(h * dk, (h + 1) * dk)
                a = _dot_nt(qt[:, hs], kt[:, hs]) * msk_ref[lv]
                atts[h] = a if atts[h] is None else atts[h] + a
        blast = b[CHUNK - 1:CHUNK, :]
        qe = (q * jnp.exp(b)).astype(BF16)
        kl = (k * jnp.exp(blast - b)).astype(BF16)
        dec = jnp.exp(blast)
        for h in range(N_HEADS):
            hs = slice(h * dk, (h + 1) * dk)
            vs = slice(2 * kw + h * dv, 2 * kw + (h + 1) * dv)
            gs = slice(2 * kw + N_HEADS * dv + h * dv, 2 * kw + N_HEADS * dv + (h + 1) * dv)
            v = p_ref[rows, vs]
            st = s_ref[h]
            o = _dot(atts[h].astype(BF16), v.astype(BF16)) + _dot_nt(qe[:, hs], st.astype(BF16))
            s_ref[h] = st * dec[:, hs] + _dot(jnp.transpose(v).astype(BF16), kl[:, hs])
            on = o * lax.rsqrt(jnp.mean(o * o, axis=-1, keepdims=True) + EPS) * nw_ref[...]
            o_ref[rows, h * dv:(h + 1) * dv] = (on * _silu(p_ref[rows, gs])).astype(BF16)

    def group(i, carry):
        for u in range(GLA_GROUP):
            chunk(i * GLA_GROUP + u)
        return carry

    lax.fori_loop(0, tb // (CHUNK * GLA_GROUP), group, 0)


def _gla_mixer(proj, small, w_lr, b_lr, norm_w, consts, B, T):
    tri, _, sel_all, masks = consts
    M = proj.shape[0]
    tb = 512
    nt = T // tb
    kw = N_HEADS * GLA_KEY_DIM
    return pl.pallas_call(
        _gla_kernel,
        grid=(B, nt),
        in_specs=[pl.BlockSpec((tb, 1536), lambda b, t: (b * nt + t, P_GLA // 1536)),
                  pl.BlockSpec((tb, LANES), lambda b, t: (b * nt + t, 0)),
                  pl.BlockSpec((GLA_GATE_RANK, kw), lambda b, t: (0, 0)),
                  pl.BlockSpec((1, kw), lambda b, t: (0, 0)),
                  pl.BlockSpec((1, HEAD_DIM), lambda b, t: (0, 0)),
                  pl.BlockSpec((CHUNK, CHUNK), lambda b, t: (0, 0)),
                  pl.BlockSpec(sel_all.shape, lambda b, t: (0, 0)),
                  pl.BlockSpec(masks.shape, lambda b, t: (0, 0, 0))],
        out_specs=pl.BlockSpec((tb, GROUP_WIDTH), lambda b, t: (b * nt + t, 0)),
        out_shape=jax.ShapeDtypeStruct((M, GROUP_WIDTH), BF16),
        scratch_shapes=[pltpu.VMEM((N_HEADS, HEAD_DIM, GLA_KEY_DIM), F32)],
        compiler_params=_cparams(2),
        name="gla_mixer",
    )(proj, small, w_lr, b_lr.reshape(1, kw), norm_w.reshape(1, HEAD_DIM),
      jnp.asarray(tri, BF16), jnp.asarray(sel_all, BF16), jnp.asarray(masks, F32))


def _gdn_kernel(x_ref, g_ref, sm_ref, cw_ref, alog_ref, dtb_ref, nw_ref, tri_ref, incl_ref, strict_ref,
                e0_ref, o_ref, s_ref, halo_ref, u_ref, wq_ref, at_ref, kt_ref, gl_ref):
    tb = x_ref.shape[0]
    d = HEAD_DIM
    hw = N_HEADS * d
    G = GDN_GROUP
    R = G * CHUNK
    HB = N_HEADS * CHUNK

    @pl.when(pl.program_id(1) == 0)
    def _():
        s_ref[...] = jnp.zeros_like(s_ref)
        halo_ref[...] = jnp.zeros_like(halo_ref)

    def stack(a):
        w = a.shape[1] // N_HEADS
        return jnp.concatenate([a[:, h * w:(h + 1) * w].reshape(G, CHUNK, w) for h in range(N_HEADS)], axis=1)

    def gate_col(a, lane, last=False):
        cols = []
        for h in range(N_HEADS):
            col = a[:, lane + h:lane + h + 1].reshape(G, CHUNK, 1)
            cols.append(jnp.broadcast_to(col[:, CHUNK - 1:CHUNK, :], (G, CHUNK, 1)) if last else col)
        return jnp.concatenate(cols, axis=1)

    incl = incl_ref[...]
    strict = strict_ref[...]
    head_of_row = lax.broadcasted_iota(jnp.int32, (HB, d), 0) // CHUNK

    def block_diag(a):
        return jnp.concatenate([jnp.where(head_of_row == h, a, 0.0) for h in range(N_HEADS)], axis=2)

    def prepare(i, carry):
        r0 = pl.multiple_of(i * R, R)
        rows = pl.ds(r0, R)
        prev = x_ref[pl.ds(pl.multiple_of(jnp.maximum(r0 - 8, 0), 8), 8), :]
        prev = jnp.where(i == 0, halo_ref[...], prev)
        win = jnp.concatenate([prev, x_ref[rows, :]], axis=0)
        conv = win * cw_ref[GDN_CONV - 1:GDN_CONV, :]
        for s in range(1, GDN_CONV):
            conv = conv + pltpu.roll(win, s, axis=0) * cw_ref[GDN_CONV - 1 - s:GDN_CONV - s, :]
        qkv = _silu(conv[8:, :])

        sm = sm_ref[rows, :]
        beta = jax.nn.sigmoid(sm)
        g = -jnp.exp(alog_ref[...]) * _softplus(sm + dtb_ref[...])
        gcum = _dot_sel(tri_ref[...], g)
        for j in range(G):
            gl_ref[i * G + j] = jnp.broadcast_to(jnp.exp(gcum[(j + 1) * CHUNK - 1:(j + 1) * CHUNK, :]), (8, LANES))

        qs, ks, vs = stack(qkv[:, 0:hw]), stack(qkv[:, hw:2 * hw]), stack(qkv[:, 2 * hw:3 * hw])
        qn = qs * lax.rsqrt(jnp.sum(qs * qs, axis=-1, keepdims=True) + EPS) * d ** -0.5
        kn = ks * lax.rsqrt(jnp.sum(ks * ks, axis=-1, keepdims=True) + EPS)
        bcol = gate_col(beta, SM_GDN_B)
        gcol = gate_col(gcum, SM_GDN_A)
        e0 = jnp.broadcast_to(e0_ref[...][None], (G, HB, d))
        gmat = jnp.broadcast_to(gcol, (G, HB, d))
        g1 = gmat.astype(BF16)
        r1 = gmat - g1.astype(F32)
        g2 = r1.astype(BF16)
        g3 = (r1 - g2.astype(F32)).astype(BF16)
        grow = _bdot_nt(e0, g1) + _bdot_nt(e0, g2) + _bdot_nt(e0, g3)
        decay = jnp.exp(jnp.minimum(gcol - grow, 0.0)) * incl
        kb = kn * bcol
        kn16 = kn.astype(BF16)
        x = -(_bdot_nt(kb.astype(BF16), kn16) * decay * strict)
        tinv = (incl - strict) + x
        for _ in range(5):
            x16 = x.astype(BF16)
            x = _bdot(x16, x16)
            tinv = tinv + _bdot(x.astype(BF16), tinv.astype(BF16))
        rhs = jnp.concatenate([vs * bcol, kb * jnp.exp(gcol)], axis=2)
        sol = _bdot(tinv.astype(BF16), rhs.astype(BF16))
        attn = _bdot_nt(qn.astype(BF16), kn16) * decay
        qe = qn * jnp.exp(gcol)
        kdec = kn * jnp.exp(gate_col(gcum, SM_GDN_A, last=True) - gcol)
        wq = jnp.concatenate([block_diag(sol[:, :, d:]), block_diag(qe)], axis=1).astype(BF16)
        kd = block_diag(kdec)
        for j in range(G):
            c = i * G + j
            u_ref[c] = sol[j, :, :d]
            wq_ref[c] = wq[j]
            at_ref[c] = attn[j].astype(BF16)
            kt_ref[c] = jnp.transpose(kd[j]).astype(BF16)
        return carry

    lax.fori_loop(0, tb // R, prepare, 0)
    halo_ref[...] = x_ref[tb - 8:tb, :]

    def recur(c, carry):
        rows = pl.ds(pl.multiple_of(c * CHUNK, CHUNK), CHUNK)
        st = s_ref[...]
        st16 = st.astype(BF16)
        ws = _dot(wq_ref[c], st16)
        v16 = (u_ref[c] - ws[:HB]).astype(BF16)
        o = ws[HB:] + _dot(at_ref[c], v16)
        gl = gl_ref[c]
        dec = jnp.concatenate([jnp.broadcast_to(gl[0:1, SM_GDN_A + h:SM_GDN_A + h + 1], (d, d))
                               for h in range(N_HEADS)], axis=0)
        s_ref[...] = st * dec + _dot(kt_ref[c], v16)
        for h in range(N_HEADS):
            hs = slice(h * d, (h + 1) * d)
            oh = o[h * CHUNK:(h + 1) * CHUNK, :]
            on = oh * lax.rsqrt(jnp.mean(oh * oh, axis=-1, keepdims=True) + EPS) * nw_ref[...]
            o_ref[rows, hs] = (on * _silu(g_ref[rows, hs])).astype(BF16)
        return carry

    lax.fori_loop(0, tb // CHUNK, recur, 0)


def _gdn_mixer(proj, small, conv_w, a_log, dt_bias, norm_w, consts, B, T):
    tri, strict, _, _ = consts
    M = proj.shape[0]
    tb = 512
    nt = T // tb
    nc = tb // CHUNK
    hb = N_HEADS * CHUNK
    pad = lambda v: jnp.zeros((1, LANES), F32).at[0, SM_GDN_A:SM_GDN_A + N_HEADS].set(v)
    tri_g = np.kron(np.eye(GDN_GROUP, dtype=np.float32), tri)
    incl_bd = np.kron(np.eye(N_HEADS, dtype=np.float32), tri)
    strict_bd = np.kron(np.eye(N_HEADS, dtype=np.float32), strict)
    e0 = np.zeros((hb, HEAD_DIM), np.float32)
    e0[:, 0] = 1.0
    const = lambda a: pl.BlockSpec(a.shape, lambda b, t: (0,) * a.ndim)
    return pl.pallas_call(
        _gdn_kernel,
        grid=(B, nt),
        in_specs=[pl.BlockSpec((tb, 1536), lambda b, t: (b * nt + t, P_GDN // 1536)),
                  pl.BlockSpec((tb, GROUP_WIDTH), lambda b, t: (b * nt + t, P_GDN_G // GROUP_WIDTH)),
                  pl.BlockSpec((tb, LANES), lambda b, t: (b * nt + t, 0)),
                  pl.BlockSpec((GDN_CONV, 1536), lambda b, t: (0, 0)),
                  pl.BlockSpec((1, LANES), lambda b, t: (0, 0)),
                  pl.BlockSpec((1, LANES), lambda b, t: (0, 0)),
                  pl.BlockSpec((1, HEAD_DIM), lambda b, t: (0, 0)),
                  const(tri_g), const(incl_bd), const(strict_bd), const(e0)],
        out_specs=pl.BlockSpec((tb, GROUP_WIDTH), lambda b, t: (b * nt + t, 0)),
        out_shape=jax.ShapeDtypeStruct((M, GROUP_WIDTH), BF16),
        scratch_shapes=[pltpu.VMEM((N_HEADS * HEAD_DIM, HEAD_DIM), F32),
                        pltpu.VMEM((8, 1536), F32),
                        pltpu.VMEM((nc, hb, HEAD_DIM), F32),
                        pltpu.VMEM((nc, 2 * hb, N_HEADS * HEAD_DIM), BF16),
                        pltpu.VMEM((nc, hb, hb), BF16),
                        pltpu.VMEM((nc, N_HEADS * HEAD_DIM, hb), BF16),
                        pltpu.VMEM((nc, 8, LANES), F32)],
        compiler_params=_cparams(2),
        name="gdn_mixer",
    )(proj, proj, small, conv_w, pad(a_log), pad(dt_bias), norm_w.reshape(1, HEAD_DIM),
      jnp.asarray(tri_g, BF16), jnp.asarray(incl_bd, F32), jnp.asarray(strict_bd, F32), jnp.asarray(e0, BF16))


POOL_HALO = 16


def _pool_kernel(u_ref, w_ref, sc_ref, o_ref, halo_ref):
    tb = u_ref.shape[0]
    gd = GROUP_WIDTH // len(POOL_WINDOWS)
    t = pl.program_id(1)

    @pl.when(t == 0)
    def _():
        halo_ref[...] = jnp.zeros_like(halo_ref)

    pos = (t * tb + 1 + lax.broadcasted_iota(jnp.int32, (tb, 1), 0)).astype(F32)
    for gi, win in enumerate(POOL_WINDOWS):
        cols = slice(gi * gd, (gi + 1) * gd)
        u = u_ref[:, cols]
        ssum = jnp.concatenate([halo_ref[:, cols], u], axis=0)
        shift = 1
        while shift < win:
            ssum = ssum + pltpu.roll(ssum, shift, axis=0)
            shift *= 2
        mean = ssum[POOL_HALO:, :] / jnp.minimum(pos, float(win))
        y = _dot((mean - u).astype(BF16), w_ref[gi])
        o_ref[:, cols] = (y * sc_ref[:, cols]).astype(BF16)
    halo_ref[...] = u_ref[tb - POOL_HALO:tb, :]


def _pool_mixer(proj, w_bf16, scale, B, T):
    M = proj.shape[0]
    tb = 512
    nt = T // tb
    gd = GROUP_WIDTH // len(POOL_WINDOWS)
    return pl.pallas_call(
        _pool_kernel,
        grid=(B, nt),
        in_specs=[pl.BlockSpec((tb, GROUP_WIDTH), lambda b, t: (b * nt + t, P_POOL // GROUP_WIDTH)),
                  pl.BlockSpec((len(POOL_WINDOWS), gd, gd), lambda b, t: (0, 0, 0)),
                  pl.BlockSpec((1, GROUP_WIDTH), lambda b, t: (0, 0))],
        out_specs=pl.BlockSpec((tb, GROUP_WIDTH), lambda b, t: (b * nt + t, 0)),
        out_shape=jax.ShapeDtypeStruct((M, GROUP_WIDTH), BF16),
        scratch_shapes=[pltpu.VMEM((POOL_HALO, GROUP_WIDTH), F32)],
        compiler_params=_cparams(2),
        name="pool_mixer",
    )(proj, w_bf16, scale.reshape(1, GROUP_WIDTH))


FOX_CUM_BLOCK = 256


def _fox_gate_kernel(sm_ref, bias_ref, tri_ref, o_ref, carry_ref):
    @pl.when(pl.program_id(1) == 0)
    def _():
        carry_ref[...] = jnp.zeros_like(carry_ref)

    lf = _log_sigmoid(sm_ref[...] + bias_ref[...])
    cum = _dot_sel(tri_ref[...], lf) + carry_ref[...]
    carry_ref[...] = cum[FOX_CUM_BLOCK - 1:FOX_CUM_BLOCK, :]
    o_ref[0] = jnp.transpose(cum)[SM_FOX_F:SM_FOX_F + 8, :]


def _fox_gate_cumsum(small, f_bias, B, T):
    tb = FOX_CUM_BLOCK
    nt = T // tb
    tri = jnp.asarray(np.tril(np.ones((tb, tb), np.float32)), BF16)
    bias = jnp.zeros((1, LANES), F32).at[0, SM_FOX_F:SM_FOX_F + N_HEADS].set(f_bias)
    return pl.pallas_call(
        _fox_gate_kernel,
        grid=(B, nt),
        in_specs=[pl.BlockSpec((tb, LANES), lambda b, t: (b * nt + t, 0)),
                  pl.BlockSpec((1, LANES), lambda b, t: (0, 0)),
                  pl.BlockSpec((tb, tb), lambda b, t: (0, 0))],
        out_specs=pl.BlockSpec((1, 8, tb), lambda b, t: (b, 0, t)),
        out_shape=jax.ShapeDtypeStruct((B, 8, T), F32),
        scratch_shapes=[pltpu.VMEM((1, LANES), F32)],
        compiler_params=_cparams(2),
        name="fox_gate_cumsum",
    )(small, bias, tri)


FOX_BLOCK = 512


def _fox_kernel(q_ref, k_ref, v_ref, f_ref, o_ref, kb_ref, vb_ref):
    qi = pl.program_id(2)
    blk = FOX_BLOCK
    d = HEAD_DIM

    @pl.when(qi == 0)
    def _():
        kb_ref[...] = k_ref[...].astype(BF16)
        vb_ref[...] = v_ref[...].astype(BF16)

    q = (q_ref[...] * d ** -0.5).astype(BF16)
    q0 = pl.multiple_of(qi * blk, blk)
    f0 = f_ref[0, :, pl.ds(q0, LANES)][:, 0:1]

    def block(ki, carry, masked):
        m, l, acc = carry
        k0 = pl.multiple_of(ki * blk, blk)
        fk = f_ref[0, :, pl.ds(k0, blk)] - f0
        s = _dot_nt(q, kb_ref[pl.ds(k0, blk), :]) - fk
        if masked:
            row = lax.broadcasted_iota(jnp.int32, (blk, blk), 0)
            col = lax.broadcasted_iota(jnp.int32, (blk, blk), 1)
            s = jnp.where(row >= col, s, -jnp.inf)
        m_new = jnp.maximum(m, jnp.max(s, axis=-1, keepdims=True))
        alpha = jnp.exp(m - m_new)
        p = jnp.exp(s - m_new)
        l = alpha * l + jnp.sum(p, axis=-1, keepdims=True)
        acc = alpha * acc + _dot(p.astype(BF16), vb_ref[pl.ds(k0, blk), :])
        return m_new, l, acc

    init = (jnp.full((blk, 1), -jnp.inf, F32), jnp.zeros((blk, 1), F32), jnp.zeros((blk, d), F32))
    carry = lax.fori_loop(0, qi, lambda ki, c: block(ki, c, False), init)
    _, l, acc = block(qi, carry, True)
    o_ref[...] = (acc / l).astype(BF16)


def _fox_attention(proj, fcum, B, T):
    M = proj.shape[0]
    blk = FOX_BLOCK
    nq = T // blk
    d = HEAD_DIM
    H = N_HEADS
    return pl.pallas_call(
        _fox_kernel,
        grid=(B, H, nq),
        in_specs=[pl.BlockSpec((blk, d), lambda b, h, i: (b * nq + i, P_FOX // d + h)),
                  pl.BlockSpec((T, d), lambda b, h, i: (b, P_FOX // d + H + h)),
                  pl.BlockSpec((T, d), lambda b, h, i: (b, P_FOX // d + 2 * H + h)),
                  pl.BlockSpec((1, 1, T), lambda b, h, i: (b * 8 + h, 0, 0))],
        out_specs=pl.BlockSpec((blk, d), lambda b, h, i: (b * nq + i, h)),
        out_shape=jax.ShapeDtypeStruct((M, GROUP_WIDTH), BF16),
        scratch_shapes=[pltpu.VMEM((T, d), BF16), pltpu.VMEM((T, d), BF16)],
        compiler_params=_cparams(3),
        name="fox_attention",
    )(proj, proj, proj, fcum.reshape(B * 8, 1, T))


def _outproj_kernel(ya_ref, yb_ref, yc_ref, yd_ref, w_ref, x_ref, mod_ref, o_ref):
    gw = GROUP_WIDTH
    acc = _dot(ya_ref[...], w_ref[0:gw, :])
    acc = acc + _dot(yb_ref[...], w_ref[gw:2 * gw, :])
    acc = acc + _dot(yc_ref[...], w_ref[2 * gw:3 * gw, :])
    acc = acc + _dot(yd_ref[...], w_ref[3 * gw:4 * gw, :])
    o_ref[...] = x_ref[...] + mod_ref[0, 2:3, :] * acc


def _out_projection(ys, w_bf16, x2, mod, T):
    M, D = x2.shape
    tm, tn = 512, 1024
    tpb = T // tm
    yspec = pl.BlockSpec((tm, GROUP_WIDTH), lambda i, j: (i, 0))
    return pl.pallas_call(
        _outproj_kernel,
        grid=(M // tm, D // tn),
        in_specs=[yspec, yspec, yspec, yspec,
                  pl.BlockSpec((D, tn), lambda i, j: (0, j)),
                  pl.BlockSpec((tm, tn), lambda i, j: (i, j)),
                  pl.BlockSpec((1, 6, tn), lambda i, j: (i // tpb, 0, j))],
        out_specs=pl.BlockSpec((tm, tn), lambda i, j: (i, j)),
        out_shape=jax.ShapeDtypeStruct((M, D), F32),
        compiler_params=_cparams(2),
        name="out_projection",
    )(*ys, w_bf16, x2, mod)


def _ffn_kernel(x_ref, mod_ref, nw_ref, wg_ref, wu_ref, cw_ref, cb_ref, wd_ref, o_ref,
                h_ref, halo_ref, *, tiles_per_seq):
    i = pl.program_id(0)
    j = pl.program_id(1)
    tm, tf = x_ref.shape[0], wg_ref.shape[2]

    @pl.when(j == 0)
    def _():
        x = x_ref[...]
        h = _rms_mod(x, nw_ref[...], mod_ref[0, 3:4, :], mod_ref[0, 4:5, :])
        h_ref[...] = h.astype(BF16)
        o_ref[...] = x

    h = h_ref[...]
    gate = _dot(h, wg_ref[0])
    up = _dot(h, wu_ref[0])
    prev = jnp.where(i % tiles_per_seq == 0, 0.0, halo_ref[j])
    row = lax.broadcasted_iota(jnp.int32, (tm, tf), 0)
    g1 = jnp.where(row == 0, prev[7:8, :], pltpu.roll(gate, 1, axis=0))
    g2 = jnp.where(row == 0, prev[6:7, :], jnp.where(row == 1, prev[7:8, :], pltpu.roll(gate, 2, axis=0)))
    halo_ref[j] = gate[tm - 8:tm, :]
    conv = cw_ref[0, 0:1, :] * g2 + cw_ref[0, 1:2, :] * g1 + cw_ref[0, 2:3, :] * gate + cb_ref[0]
    act = (_silu(conv) * up).astype(BF16)
    o_ref[...] += mod_ref[0, 5:6, :] * _dot(act, wd_ref[0])


def _conv_ffn(x2, mod, norm_w, wg, wu, conv_w, conv_b, wd, l, T):
    M, D = x2.shape
    L, _, F = wg.shape
    tm, tf = 1024, 512
    tpb = T // tm
    return pl.pallas_call(
        functools.partial(_ffn_kernel, tiles_per_seq=tpb),
        grid=(M // tm, F // tf),
        in_specs=[pl.BlockSpec((tm, D), lambda i, j: (i, 0), pipeline_mode=pl.Buffered(1)),
                  pl.BlockSpec((1, 6, D), lambda i, j: (i // tpb, 0, 0)),
                  pl.BlockSpec((1, D), lambda i, j: (0, 0)),
                  pl.BlockSpec((1, D, tf), lambda i, j: (l, 0, j)),
                  pl.BlockSpec((1, D, tf), lambda i, j: (l, 0, j)),
                  pl.BlockSpec((1, FFN_CONV, tf), lambda i, j: (l, 0, j)),
                  pl.BlockSpec((1, 1, tf), lambda i, j: (l, 0, j)),
                  pl.BlockSpec((1, tf, D), lambda i, j: (l, j, 0))],
        out_specs=pl.BlockSpec((tm, D), lambda i, j: (i, 0)),
        out_shape=jax.ShapeDtypeStruct((M, D), F32),
        scratch_shapes=[pltpu.VMEM((tm, D), BF16),
                        pltpu.VMEM((F // tf, 8, tf), F32)],
        compiler_params=pltpu.CompilerParams(dimension_semantics=("arbitrary", "arbitrary"),
                                             vmem_limit_bytes=FFN_VMEM_LIMIT),
        name="conv_ffn",
    )(x2, mod, norm_w.reshape(1, D), wg, wu, conv_w, conv_b.reshape(L, 1, F), wd)


def _final_norm_kernel(x_ref, w_ref, o_ref):
    x = x_ref[...]
    o_ref[...] = x * lax.rsqrt(jnp.mean(x * x, axis=-1, keepdims=True) + EPS) * w_ref[...]


def _final_norm(x2, w):
    M, D = x2.shape
    tm = 512
    return pl.pallas_call(
        _final_norm_kernel,
        grid=(M // tm,),
        in_specs=[pl.BlockSpec((tm, D), lambda i: (i, 0)), pl.BlockSpec((1, D), lambda i: (0, 0))],
        out_specs=pl.BlockSpec((tm, D), lambda i: (i, 0)),
        out_shape=jax.ShapeDtypeStruct((M, D), F32),
        compiler_params=_cparams(1),
        name="final_norm",
    )(x2, w.reshape(1, D))


def kernel(x, c, w_mod, b_mod, norm_mix, norm_ffn, w_in, gla_w_lr, gla_b_lr, gla_norm, gdn_conv, gdn_a_log, gdn_dt_bias, gdn_norm, pool_w, pool_scale, fox_f_bias, w_out, ffn_w_gate, ffn_w_up, ffn_conv_w, ffn_conv_b, ffn_w_down, norm_final):
    B, T, D = x.shape
    L = w_mod.shape[0]
    assert D == D_MODEL and T % 1024 == 0 and B <= 8
    consts = _chunk_constants()

    c_pad = jnp.zeros((8, D), F32).at[:B].set(c)
    mod_all = _modulation(c_pad, w_mod, b_mod)
    w_main, w_small = _prepare_in_weights(w_in)
    w_gate, w_up, w_down = ffn_w_gate.astype(BF16), ffn_w_up.astype(BF16), ffn_w_down.astype(BF16)
    x2 = x.reshape(B * T, D)
    for l in range(L):
        mod = mod_all[l, :B].reshape(B, 6, D)
        proj, small = _in_projection(x2, mod, norm_mix[l], w_main, w_small, l, T)
        y_a = _gla_mixer(proj, small, gla_w_lr[l], gla_b_lr[l], gla_norm[l], consts, B, T)
        y_b = _gdn_mixer(proj, small, gdn_conv[l], gdn_a_log[l], gdn_dt_bias[l], gdn_norm[l], consts, B, T)
        y_c = _pool_mixer(proj, pool_w[l].astype(BF16), pool_scale[l], B, T)
        y_d = _fox_attention(proj, _fox_gate_cumsum(small, fox_f_bias[l], B, T), B, T)
        x2 = _out_projection((y_a, y_b, y_c, y_d), w_out[l].astype(BF16), x2, mod, T)
        x2 = _conv_ffn(x2, mod, norm_ffn[l], w_gate, w_up, ffn_conv_w, ffn_conv_b, w_down, l, T)
    return _final_norm(x2, norm_final).reshape(B, T, D)
```

```python
import functools
import math

import numpy as np
import jax
import jax.numpy as jnp
from jax import lax
from jax.experimental import pallas as pl
from jax.experimental.pallas import tpu as pltpu

F32 = jnp.float32
BF16 = jnp.bfloat16
HIGHEST = lax.Precision.HIGHEST

D_MODEL = 2048
GROUP_WIDTH = D_MODEL // 4
N_HEADS = 4
HEAD_DIM = GROUP_WIDTH // 4
GLA_KEY_DIM = HEAD_DIM // 2
GLA_GATE_RANK = 16
GLA_GATE_TAU = 16.0
GDN_CONV = 4
POOL_WINDOWS = (2, 4, 8, 16)
CHUNK = 64
D_FF = 256 * int(math.ceil(8 * D_MODEL / 3 / 256))
FFN_CONV = 3
EPS = 1e-6

IN_SPLITS = (256, 256, 512, 512, 16, 1536, 512, 4, 4, 512, 512, 512, 512, 4)
IN_OFFS = tuple(int(sum(IN_SPLITS[:i])) for i in range(len(IN_SPLITS) + 1))

LANES = 128
P_FOX = 0
P_GLA = 1536
P_GDN = 3072
P_GDN_G = 4608
P_POOL = 5120
P_WIDTH = 5632
SM_GLA_LR = 0
SM_GDN_B = 16
SM_GDN_A = 20
SM_FOX_F = 28
FOX_ROW0 = SM_FOX_F // 8 * 8

VMEM_LIMIT = 56 * 1024 * 1024
FFN_VMEM_LIMIT = 60 * 1024 * 1024


def _cparams(n_axes):
    return pltpu.CompilerParams(dimension_semantics=("arbitrary",) * n_axes,
                                vmem_limit_bytes=VMEM_LIMIT)


def _dot(a, b, precision=None):
    return jnp.dot(a, b, preferred_element_type=F32, precision=precision)


def _dot_nt(a, b):
    return lax.dot_general(a, b, (((1,), (1,)), ((), ())), preferred_element_type=F32)


def _bdot(a, b):
    return lax.dot_general(a, b, (((2,), (1,)), ((0,), (0,))), preferred_element_type=F32)


def _bdot_nt(a, b):
    return lax.dot_general(a, b, (((2,), (2,)), ((0,), (0,))), preferred_element_type=F32)


def _dot_sel(sel_bf16, x):
    x1 = x.astype(BF16)
    r1 = x - x1.astype(F32)
    x2 = r1.astype(BF16)
    x3 = (r1 - x2.astype(F32)).astype(BF16)
    return _dot(sel_bf16, x1) + _dot(sel_bf16, x2) + _dot(sel_bf16, x3)


def _log_sigmoid(x):
    return jnp.minimum(x, 0.0) - jnp.log1p(jnp.exp(-jnp.abs(x)))


def _softplus(x):
    return jnp.maximum(x, 0.0) + jnp.log1p(jnp.exp(-jnp.abs(x)))


def _silu(x):
    return x * jax.nn.sigmoid(x)


def _rms_mod(x, norm_w, shift, scale):
    y = x * lax.rsqrt(jnp.mean(x * x, axis=-1, keepdims=True) + EPS) * norm_w
    return y * (1.0 + scale) + shift


def _mod_kernel(c_ref, w_ref, b_ref, o_ref):
    cond = _silu(c_ref[...])
    o_ref[0] = _dot(cond, w_ref[0], precision=HIGHEST) + b_ref[0]


def _modulation(c_pad, w_mod, b_mod):
    L, D, N = w_mod.shape
    tn = 1024
    return pl.pallas_call(
        _mod_kernel,
        grid=(L, N // tn),
        in_specs=[pl.BlockSpec((8, D), lambda l, j: (0, 0)),
                  pl.BlockSpec((1, D, tn), lambda l, j: (l, 0, j)),
                  pl.BlockSpec((1, 1, tn), lambda l, j: (l, 0, j))],
        out_specs=pl.BlockSpec((1, 8, tn), lambda l, j: (l, 0, j)),
        out_shape=jax.ShapeDtypeStruct((L, 8, N), F32),
        compiler_params=_cparams(2),
        name="modulation",
    )(c_pad, w_mod, b_mod.reshape(L, 1, N))


def _wprep_kernel(w_ref, main_ref, small_ref):
    tk = w_ref.shape[2]
    nf = IN_SPLITS[13]
    assert (SM_GLA_LR, SM_GDN_B, SM_GDN_A, SM_FOX_F) == (0, 16, 20, 28) and nf == 4
    for l in range(w_ref.shape[1]):
        off = 0
        for i in (10, 11, 12, 0, 1, 2, 3, 5, 6, 9):
            width = IN_SPLITS[i]
            main_ref[l, :, off:off + width] = jnp.transpose(w_ref[IN_OFFS[i]:IN_OFFS[i + 1], l, :]).astype(BF16)
            off += width
        tail = w_ref[IN_OFFS[14] - 8:IN_OFFS[14], l, :]
        tail = jnp.where(lax.broadcasted_iota(jnp.int32, (8, tk), 0) >= 8 - nf, tail, 0.0)
        narrow = jnp.concatenate(
            [w_ref[IN_OFFS[4]:IN_OFFS[5], l, :],
             w_ref[IN_OFFS[7]:IN_OFFS[9], l, :],
             tail,
             jnp.zeros((LANES - 32, tk), F32)], axis=0)
        small_ref[l] = jnp.transpose(narrow).astype(BF16)


def _prepare_in_weights(w_in):
    L, D, N = w_in.shape
    tk = 128
    wt = jnp.transpose(w_in, (2, 0, 1))
    return pl.pallas_call(
        _wprep_kernel,
        grid=(D // tk,),
        in_specs=[pl.BlockSpec((N, L, tk), lambda r: (0, 0, r))],
        out_specs=[pl.BlockSpec((L, tk, P_WIDTH), lambda r: (0, r, 0)),
                   pl.BlockSpec((L, tk, LANES), lambda r: (0, r, 0))],
        out_shape=[jax.ShapeDtypeStruct((L, D, P_WIDTH), BF16),
                   jax.ShapeDtypeStruct((L, D, LANES), BF16)],
        compiler_params=_cparams(1),
        name="in_weight_prep",
    )(wt)


def _inproj_kernel(x_ref, mod_ref, nw_ref, w_ref, ws_ref, o_ref, os_ref, h_ref):
    @pl.when(pl.program_id(1) == 0)
    def _():
        h = _rms_mod(x_ref[...], nw_ref[...], mod_ref[0, 0:1, :], mod_ref[0, 1:2, :])
        h_ref[...] = h.astype(BF16)
        os_ref[...] = _dot(h_ref[...], ws_ref[0])

    o_ref[...] = _dot(h_ref[...], w_ref[0])


def _in_projection(x2, mod, norm_w, w_main, w_small, l, T):
    M, D = x2.shape
    N = w_main.shape[2]
    tm, tn = 1024, 512
    tpb = T // tm
    return pl.pallas_call(
        _inproj_kernel,
        grid=(M // tm, N // tn),
        in_specs=[pl.BlockSpec((tm, D), lambda i, j: (i, 0)),
                  pl.BlockSpec((1, 6, D), lambda i, j: (i // tpb, 0, 0)),
                  pl.BlockSpec((1, D), lambda i, j: (0, 0)),
                  pl.BlockSpec((1, D, tn), lambda i, j: (l, 0, j)),
                  pl.BlockSpec((1, D, LANES), lambda i, j: (l, 0, 0))],
        out_specs=[pl.BlockSpec((tm, tn), lambda i, j: (i, j)),
                   pl.BlockSpec((tm, LANES), lambda i, j: (i, 0))],
        out_shape=[jax.ShapeDtypeStruct((M, N), F32),
                   jax.ShapeDtypeStruct((M, LANES), F32)],
        scratch_shapes=[pltpu.VMEM((tm, D), BF16)],
        compiler_params=_cparams(2),
        name="in_projection",
    )(x2, mod, norm_w.reshape(1, D), w_main, w_small)


GLA_LEVELS = 7
GLA_GROUP = 2
GDN_GROUP = 4
GDN_PAIR = 2
GDN_LEVELS = 6


def _chunk_constants():
    i = np.arange(CHUNK)[:, None]
    j = np.arange(CHUNK)[None, :]
    tri = (i >= j).astype(np.float32)
    strict = (i > j).astype(np.float32)
    sels, masks = [], [(i == j).astype(np.float32)]
    for lv in range(1, GLA_LEVELS):
        m = 1 << (lv - 1)
        ref_row = (np.arange(CHUNK) // (2 * m)) * 2 * m + m
        sels.append((j == ref_row[:, None]).astype(np.float32))
        same = (i // (2 * m)) == (j // (2 * m))
        masks.append((same & ((i % (2 * m)) >= m) & ((j % (2 * m)) < m)).astype(np.float32))
    sel_all = np.concatenate(sels, axis=0)
    return tri, strict, sel_all, np.stack(masks)


def _gla_kernel(p_ref, sm_ref, wlr_ref, blr_ref, nw_ref, tri_ref, sel_ref, msk_ref, o_ref, s_ref):
    tb = p_ref.shape[0]
    dk, dv = GLA_KEY_DIM, HEAD_DIM
    kw = N_HEADS * dk

    @pl.when(pl.program_id(1) == 0)
    def _():
        s_ref[...] = jnp.zeros_like(s_ref)

    def chunk(c):
        rows = pl.ds(pl.multiple_of(c * CHUNK, CHUNK), CHUNK)
        lr = sm_ref[rows, SM_GLA_LR:SM_GLA_LR + GLA_GATE_RANK]
        logits = _dot(lr, wlr_ref[...], precision=HIGHEST) + blr_ref[...]
        logg = _log_sigmoid(logits) * (1.0 / GLA_GATE_TAU)
        b = _dot_sel(tri_ref[...], logg)
        q = p_ref[rows, 0:kw] * dk ** -0.5
        k = p_ref[rows, kw:2 * kw]
        refs = _dot_sel(sel_ref[...], b)
        atts = [None] * N_HEADS
        for lv in range(GLA_LEVELS):
            if lv == 0:
                qt, kt = q, k
            else:
                r = refs[(lv - 1) * CHUNK:lv * CHUNK, :]
                qt = q * jnp.exp(jnp.minimum(b - r, 0.0))
                kt = k * jnp.exp(jnp.minimum(r - b, 0.0))
            qt, kt = qt.astype(BF16), kt.astype(BF16)
            for h in range(N_HEADS):
                hs = slice(h * dk, (h + 1) * dk)
                a = _dot_nt(qt[:, hs], kt[:, hs]) * msk_ref[lv]
                atts[h] = a if atts[h] is None else atts[h] + a
        blast = b[CHUNK - 1:CHUNK, :]
        qe = (q * jnp.exp(b)).astype(BF16)
        kl = (k * jnp.exp(blast - b)).astype(BF16)
        dec = jnp.exp(blast)
        for h in range(N_HEADS):
            hs = slice(h * dk, (h + 1) * dk)
            vs = slice(2 * kw + h * dv, 2 * kw + (h + 1) * dv)
            gs = slice(2 * kw + N_HEADS * dv + h * dv, 2 * kw + N_HEADS * dv + (h + 1) * dv)
            v = p_ref[rows, vs]
            st = s_ref[h]
            o = _dot(atts[h].astype(BF16), v.astype(BF16)) + _dot_nt(qe[:, hs], st.astype(BF16))
            s_ref[h] = st * dec[:, hs] + _dot(jnp.transpose(v).astype(BF16), kl[:, hs])
            on = o * lax.rsqrt(jnp.mean(o * o, axis=-1, keepdims=True) + EPS) * nw_ref[...]
            o_ref[rows, h * dv:(h + 1) * dv] = (on * _silu(p_ref[rows, gs])).astype(BF16)

    def group(i, carry):
        for u in range(GLA_GROUP):
            chunk(i * GLA_GROUP + u)
        return carry

    lax.fori_loop(0, tb // (CHUNK * GLA_GROUP), group, 0)


def _gla_mixer(proj, small, w_lr, b_lr, norm_w, consts, B, T):
    tri, _, sel_all, masks = consts
    M = proj.shape[0]
    tb = 512
    nt = T // tb
    kw = N_HEADS * GLA_KEY_DIM
    return pl.pallas_call(
        _gla_kernel,
        grid=(B, nt),
        in_specs=[pl.BlockSpec((tb, 1536), lambda b, t: (b * nt + t, P_GLA // 1536)),
                  pl.BlockSpec((tb, LANES), lambda b, t: (b * nt + t, 0)),
                  pl.BlockSpec((GLA_GATE_RANK, kw), lambda b, t: (0, 0)),
                  pl.BlockSpec((1, kw), lambda b, t: (0, 0)),
                  pl.BlockSpec((1, HEAD_DIM), lambda b, t: (0, 0)),
                  pl.BlockSpec((CHUNK, CHUNK), lambda b, t: (0, 0)),
                  pl.BlockSpec(sel_all.shape, lambda b, t: (0, 0)),
                  pl.BlockSpec(masks.shape, lambda b, t: (0, 0, 0))],
        out_specs=pl.BlockSpec((tb, GROUP_WIDTH), lambda b, t: (b * nt + t, 0)),
        out_shape=jax.ShapeDtypeStruct((M, GROUP_WIDTH), BF16),
        scratch_shapes=[pltpu.VMEM((N_HEADS, HEAD_DIM, GLA_KEY_DIM), F32)],
        compiler_params=_cparams(2),
        name="gla_mixer",
    )(proj, small, w_lr, b_lr.reshape(1, kw), norm_w.reshape(1, HEAD_DIM),
      jnp.asarray(tri, BF16), jnp.asarray(sel_all, BF16), jnp.asarray(masks, F32))


def _gdn_kernel(x_ref, g_ref, sm_ref, cw_ref, alog_ref, dtb_ref, nw_ref, tri_ref, incl_ref, lvl_ref,
                o_ref, s_ref, halo_ref, cq_ref, ku_ref, au_ref, gl_ref):
    tb = x_ref.shape[0]
    d = HEAD_DIM
    hw = N_HEADS * d
    G = GDN_GROUP
    R = G * CHUNK
    P = GDN_PAIR
    BR = P * CHUNK
    NP = N_HEADS // P

    @pl.when(pl.program_id(1) == 0)
    def _():
        s_ref[...] = jnp.zeros_like(s_ref)
        halo_ref[...] = jnp.zeros_like(halo_ref)

    def stack(a):
        w = a.shape[1] // N_HEADS
        head = lambda h: a[:, h * w:(h + 1) * w].reshape(G, CHUNK, w)
        return jnp.concatenate([jnp.concatenate([head(p * P + j) for j in range(P)], axis=1)
                                for p in range(NP)], axis=0)

    def gate_col(a, lane, last=False):
        def head(h):
            col = a[:, lane + h:lane + h + 1].reshape(G, CHUNK, 1)
            return jnp.broadcast_to(col[:, CHUNK - 1:CHUNK, :], (G, CHUNK, 1)) if last else col
        return jnp.concatenate([jnp.concatenate([head(p * P + j) for j in range(P)], axis=1)
                                for p in range(NP)], axis=0)

    incl = incl_ref[...]
    head_of_row = lax.broadcasted_iota(jnp.int32, (BR, d), 0) // CHUNK

    def block_diag(a):
        return jnp.concatenate([jnp.where(head_of_row == j, a, 0.0) for j in range(P)], axis=2)

    def prepare(i, carry):
        r0 = pl.multiple_of(i * R, R)
        rows = pl.ds(r0, R)
        prev = x_ref[pl.ds(pl.multiple_of(jnp.maximum(r0 - 8, 0), 8), 8), :]
        prev = jnp.where(i == 0, halo_ref[...], prev)
        win = jnp.concatenate([prev, x_ref[rows, :]], axis=0)
        conv = win * cw_ref[GDN_CONV - 1:GDN_CONV, :]
        for s in range(1, GDN_CONV):
            conv = conv + pltpu.roll(win, s, axis=0) * cw_ref[GDN_CONV - 1 - s:GDN_CONV - s, :]
        qkv = _silu(conv[8:, :])

        sm = sm_ref[rows, :]
        beta = jax.nn.sigmoid(sm)
        g = -jnp.exp(alog_ref[...]) * _softplus(sm + dtb_ref[...])
        gcum = _dot_sel(tri_ref[...], g)
        for j in range(G):
            gl_ref[i * G + j] = jnp.broadcast_to(jnp.exp(gcum[(j + 1) * CHUNK - 1:(j + 1) * CHUNK, :]), (8, LANES))

        qs, ks, vs = stack(qkv[:, 0:hw]), stack(qkv[:, hw:2 * hw]), stack(qkv[:, 2 * hw:3 * hw])
        qn = qs * lax.rsqrt(jnp.sum(qs * qs, axis=-1, keepdims=True) + EPS) * d ** -0.5
        kn = ks * lax.rsqrt(jnp.sum(ks * ks, axis=-1, keepdims=True) + EPS)
        bcol = gate_col(beta, SM_GDN_B)
        gcol = gate_col(gcum, SM_GDN_A)
        gmat = jnp.broadcast_to(gcol, (NP * G, BR, BR))
        grow = jnp.stack([jnp.transpose(gmat[b]) for b in range(NP * G)])
        decay = jnp.exp(jnp.minimum(gcol - grow, 0.0)) * incl
        kb = kn * bcol
        kn16 = kn.astype(BF16)
        lmat = _bdot_nt(kb.astype(BF16), kn16) * decay
        tinv = lvl_ref[GDN_LEVELS] - lmat * lvl_ref[0]
        for lv in range(1, GDN_LEVELS):
            t16 = tinv.astype(BF16)
            tl = _bdot(t16, (lmat * lvl_ref[lv]).astype(BF16))
            tinv = tinv - _bdot(tl.astype(BF16), t16)
        rhs = jnp.concatenate([vs * bcol, kb * jnp.exp(gcol)], axis=2)
        sol = _bdot(tinv.astype(BF16), rhs.astype(BF16))
        sol16 = sol.astype(BF16)
        attn = _bdot_nt(qn.astype(BF16), kn16) * decay
        aw = _bdot(attn.astype(BF16), sol16)
        qp = (qn * jnp.exp(gcol) - aw[:, :, d:]).astype(BF16)
        kdec = kn * jnp.exp(gate_col(gcum, SM_GDN_A, last=True) - gcol)
        kd = block_diag(kdec)
        for p in range(NP):
            for g_ in range(G):
                b, c = p * G + g_, i * G + g_
                kt = jnp.transpose(kd[b]).astype(BF16)
                ku = _dot(kt, sol16[b])
                for j in range(P):
                    h = p * P + j
                    ku_ref[c, h * d:(h + 1) * d, :] = ku[j * d:(j + 1) * d, :d]
                    au_ref[c, h * CHUNK:(h + 1) * CHUNK, :] = aw[b, j * CHUNK:(j + 1) * CHUNK, :d]
                    cq_ref[c * N_HEADS + h, 0:d, :] = ku[j * d:(j + 1) * d, d:].astype(BF16)
                    cq_ref[c * N_HEADS + h, d:d + CHUNK, :] = qp[b, j * CHUNK:(j + 1) * CHUNK, :]
        return carry

    lax.fori_loop(0, tb // R, prepare, 0)
    halo_ref[...] = x_ref[tb - 8:tb, :]

    def recur(c, carry):
        rows = pl.ds(pl.multiple_of(c * CHUNK, CHUNK), CHUNK)
        gl = gl_ref[c]
        for h in range(N_HEADS):
            hs = slice(h * d, (h + 1) * d)
            st = s_ref[h]
            r = _dot(cq_ref[c * N_HEADS + h], st.astype(BF16))
            s_ref[h] = st * gl[0:1, SM_GDN_A + h:SM_GDN_A + h + 1] - r[:d] + ku_ref[c, hs, :]
            oh = r[d:] + au_ref[c, h * CHUNK:(h + 1) * CHUNK, :]
            on = oh * lax.rsqrt(jnp.mean(oh * oh, axis=-1, keepdims=True) + EPS) * nw_ref[...]
            o_ref[rows, hs] = (on * _silu(g_ref[rows, hs])).astype(BF16)
        return carry

    lax.fori_loop(0, tb // CHUNK, recur, 0)


def _gdn_mixer(proj, small, conv_w, a_log, dt_bias, norm_w, consts, B, T):
    tri, _, _, masks = consts
    M = proj.shape[0]
    tb = 512
    nt = T // tb
    nc = tb // CHUNK
    hb = N_HEADS * CHUNK
    pad = lambda v: jnp.zeros((1, LANES), F32).at[0, SM_GDN_A:SM_GDN_A + N_HEADS].set(v)
    tri_g = np.kron(np.eye(GDN_GROUP, dtype=np.float32), tri)
    incl_bd = np.kron(np.eye(GDN_PAIR, dtype=np.float32), tri)
    eye_p = np.eye(GDN_PAIR, dtype=np.float32)
    lvl = np.stack([np.kron(eye_p, m) for m in masks[1:]] + [np.eye(GDN_PAIR * CHUNK, dtype=np.float32)])
    assert lvl.shape[0] == GDN_LEVELS + 1
    const = lambda a: pl.BlockSpec(a.shape, lambda b, t: (0,) * a.ndim)
    return pl.pallas_call(
        _gdn_kernel,
        grid=(B, nt),
        in_specs=[pl.BlockSpec((tb, 1536), lambda b, t: (b * nt + t, P_GDN // 1536)),
                  pl.BlockSpec((tb, GROUP_WIDTH), lambda b, t: (b * nt + t, P_GDN_G // GROUP_WIDTH)),
                  pl.BlockSpec((tb, LANES), lambda b, t: (b * nt + t, 0)),
                  pl.BlockSpec((GDN_CONV, 1536), lambda b, t: (0, 0)),
                  pl.BlockSpec((1, LANES), lambda b, t: (0, 0)),
                  pl.BlockSpec((1, LANES), lambda b, t: (0, 0)),
                  pl.BlockSpec((1, HEAD_DIM), lambda b, t: (0, 0)),
                  const(tri_g), const(incl_bd), const(lvl)],
        out_specs=pl.BlockSpec((tb, GROUP_WIDTH), lambda b, t: (b * nt + t, 0)),
        out_shape=jax.ShapeDtypeStruct((M, GROUP_WIDTH), BF16),
        scratch_shapes=[pltpu.VMEM((N_HEADS, HEAD_DIM, HEAD_DIM), F32),
                        pltpu.VMEM((8, 1536), F32),
                        pltpu.VMEM((nc * N_HEADS, HEAD_DIM + CHUNK, HEAD_DIM), BF16),
                        pltpu.VMEM((nc, N_HEADS * HEAD_DIM, HEAD_DIM), F32),
                        pltpu.VMEM((nc, hb, HEAD_DIM), F32),
                        pltpu.VMEM((nc, 8, LANES), F32)],
        compiler_params=_cparams(2),
        name="gdn_mixer",
    )(proj, proj, small, conv_w, pad(a_log), pad(dt_bias), norm_w.reshape(1, HEAD_DIM),
      jnp.asarray(tri_g, BF16), jnp.asarray(incl_bd, F32), jnp.asarray(lvl, F32))


POOL_HALO = 16


def _pool_kernel(u_ref, w_ref, sc_ref, o_ref, halo_ref):
    tb = u_ref.shape[0]
    gd = GROUP_WIDTH // len(POOL_WINDOWS)
    t = pl.program_id(1)

    @pl.when(t == 0)
    def _():
        halo_ref[...] = jnp.zeros_like(halo_ref)

    pos = (t * tb + 1 + lax.broadcasted_iota(jnp.int32, (tb, 1), 0)).astype(F32)
    for gi, win in enumerate(POOL_WINDOWS):
        cols = slice(gi * gd, (gi + 1) * gd)
        u = u_ref[:, cols]
        ssum = jnp.concatenate([halo_ref[:, cols], u], axis=0)
        shift = 1
        while shift < win:
            ssum = ssum + pltpu.roll(ssum, shift, axis=0)
            shift *= 2
        mean = ssum[POOL_HALO:, :] / jnp.minimum(pos, float(win))
        y = _dot((mean - u).astype(BF16), w_ref[gi])
        o_ref[:, cols] = (y * sc_ref[:, cols]).astype(BF16)
    halo_ref[...] = u_ref[tb - POOL_HALO:tb, :]


def _pool_mixer(proj, w_bf16, scale, B, T):
    M = proj.shape[0]
    tb = 512
    nt = T // tb
    gd = GROUP_WIDTH // len(POOL_WINDOWS)
    return pl.pallas_call(
        _pool_kernel,
        grid=(B, nt),
        in_specs=[pl.BlockSpec((tb, GROUP_WIDTH), lambda b, t: (b * nt + t, P_POOL // GROUP_WIDTH)),
                  pl.BlockSpec((len(POOL_WINDOWS), gd, gd), lambda b, t: (0, 0, 0)),
                  pl.BlockSpec((1, GROUP_WIDTH), lambda b, t: (0, 0))],
        out_specs=pl.BlockSpec((tb, GROUP_WIDTH), lambda b, t: (b * nt + t, 0)),
        out_shape=jax.ShapeDtypeStruct((M, GROUP_WIDTH), BF16),
        scratch_shapes=[pltpu.VMEM((POOL_HALO, GROUP_WIDTH), F32)],
        compiler_params=_cparams(2),
        name="pool_mixer",
    )(proj, w_bf16, scale.reshape(1, GROUP_WIDTH))


FOX_CUM_BLOCK = 256


def _fox_gate_kernel(sm_ref, bias_ref, tri_ref, o_ref, carry_ref):
    @pl.when(pl.program_id(1) == 0)
    def _():
        carry_ref[...] = jnp.zeros_like(carry_ref)

    lf = _log_sigmoid(sm_ref[...] + bias_ref[...])
    cum = _dot_sel(tri_ref[...], lf) + carry_ref[...]
    carry_ref[...] = cum[FOX_CUM_BLOCK - 1:FOX_CUM_BLOCK, :]
    o_ref[0] = jnp.transpose(cum)[FOX_ROW0:FOX_ROW0 + 8, :]


def _fox_gate_cumsum(small, f_bias, B, T):
    tb = FOX_CUM_BLOCK
    nt = T // tb
    tri = jnp.asarray(np.tril(np.ones((tb, tb), np.float32)), BF16)
    bias = jnp.zeros((1, LANES), F32).at[0, SM_FOX_F:SM_FOX_F + N_HEADS].set(f_bias)
    return pl.pallas_call(
        _fox_gate_kernel,
        grid=(B, nt),
        in_specs=[pl.BlockSpec((tb, LANES), lambda b, t: (b * nt + t, 0)),
                  pl.BlockSpec((1, LANES), lambda b, t: (0, 0)),
                  pl.BlockSpec((tb, tb), lambda b, t: (0, 0))],
        out_specs=pl.BlockSpec((1, 8, tb), lambda b, t: (b, 0, t)),
        out_shape=jax.ShapeDtypeStruct((B, 8, T), F32),
        scratch_shapes=[pltpu.VMEM((1, LANES), F32)],
        compiler_params=_cparams(2),
        name="fox_gate_cumsum",
    )(small, bias, tri)


FOX_BLOCK = 512


def _fox_kernel(q_ref, k_ref, v_ref, f_ref, o_ref, kb_ref, vb_ref):
    qi = pl.program_id(2)
    blk = FOX_BLOCK
    d = HEAD_DIM

    @pl.when(qi == 0)
    def _():
        kb_ref[...] = k_ref[...].astype(BF16)
        vb_ref[...] = v_ref[...].astype(BF16)

    q = (q_ref[...] * d ** -0.5).astype(BF16)
    q0 = pl.multiple_of(qi * blk, blk)
    f0 = f_ref[0, :, pl.ds(q0, LANES)][:, 0:1]

    def block(ki, carry, masked):
        m, l, acc = carry
        k0 = pl.multiple_of(ki * blk, blk)
        fk = f_ref[0, :, pl.ds(k0, blk)] - f0
        s = _dot_nt(q, kb_ref[pl.ds(k0, blk), :]) - fk
        if masked:
            row = lax.broadcasted_iota(jnp.int32, (blk, blk), 0)
            col = lax.broadcasted_iota(jnp.int32, (blk, blk), 1)
            s = jnp.where(row >= col, s, -jnp.inf)
        m_new = jnp.maximum(m, jnp.max(s, axis=-1, keepdims=True))
        alpha = jnp.exp(m - m_new)
        p = jnp.exp(s - m_new)
        l = alpha * l + jnp.sum(p, axis=-1, keepdims=True)
        acc = alpha * acc + _dot(p.astype(BF16), vb_ref[pl.ds(k0, blk), :])
        return m_new, l, acc

    init = (jnp.full((blk, 1), -jnp.inf, F32), jnp.zeros((blk, 1), F32), jnp.zeros((blk, d), F32))
    carry = lax.fori_loop(0, qi, lambda ki, c: block(ki, c, False), init)
    _, l, acc = block(qi, carry, True)
    o_ref[...] = (acc / l).astype(BF16)


def _fox_attention(proj, fcum, B, T):
    M = proj.shape[0]
    blk = FOX_BLOCK
    nq = T // blk
    d = HEAD_DIM
    H = N_HEADS
    return pl.pallas_call(
        _fox_kernel,
        grid=(B, H, nq),
        in_specs=[pl.BlockSpec((blk, d), lambda b, h, i: (b * nq + i, P_FOX // d + h)),
                  pl.BlockSpec((T, d), lambda b, h, i: (b, P_FOX // d + H + h)),
                  pl.BlockSpec((T, d), lambda b, h, i: (b, P_FOX // d + 2 * H + h)),
                  pl.BlockSpec((1, 1, T), lambda b, h, i: (b * 8 + SM_FOX_F - FOX_ROW0 + h, 0, 0))],
        out_specs=pl.BlockSpec((blk, d), lambda b, h, i: (b * nq + i, h)),
        out_shape=jax.ShapeDtypeStruct((M, GROUP_WIDTH), BF16),
        scratch_shapes=[pltpu.VMEM((T, d), BF16), pltpu.VMEM((T, d), BF16)],
        compiler_params=_cparams(3),
        name="fox_attention",
    )(proj, proj, proj, fcum.reshape(B * 8, 1, T))


def _outproj_kernel(ya_ref, yb_ref, yc_ref, yd_ref, w_ref, x_ref, mod_ref, o_ref):
    gw = GROUP_WIDTH
    acc = _dot(ya_ref[...], w_ref[0:gw, :])
    acc = acc + _dot(yb_ref[...], w_ref[gw:2 * gw, :])
    acc = acc + _dot(yc_ref[...], w_ref[2 * gw:3 * gw, :])
    acc = acc + _dot(yd_ref[...], w_ref[3 * gw:4 * gw, :])
    o_ref[...] = x_ref[...] + mod_ref[0, 2:3, :] * acc


def _out_projection(ys, w_bf16, x2, mod, T):
    M, D = x2.shape
    tm, tn = 1024, 1024
    tpb = T // tm
    yspec = pl.BlockSpec((tm, GROUP_WIDTH), lambda i, j: (i, 0))
    return pl.pallas_call(
        _outproj_kernel,
        grid=(M // tm, D // tn),
        in_specs=[yspec, yspec, yspec, yspec,
                  pl.BlockSpec((D, tn), lambda i, j: (0, j)),
                  pl.BlockSpec((tm, tn), lambda i, j: (i, j)),
                  pl.BlockSpec((1, 6, tn), lambda i, j: (i // tpb, 0, j))],
        out_specs=pl.BlockSpec((tm, tn), lambda i, j: (i, j)),
        out_shape=jax.ShapeDtypeStruct((M, D), F32),
        compiler_params=_cparams(2),
        name="out_projection",
    )(*ys, w_bf16, x2, mod)


def _ffn_kernel(x_ref, mod_ref, nw_ref, wg_ref, wu_ref, cw_ref, cb_ref, wd_ref, fw_ref, o_ref,
                h_ref, halo_ref, *, tiles_per_seq, final_norm):
    i = pl.program_id(0)
    j = pl.program_id(1)
    tm, tf = x_ref.shape[0], wg_ref.shape[2]

    @pl.when(j == 0)
    def _():
        x = x_ref[...]
        h = _rms_mod(x, nw_ref[...], mod_ref[0, 3:4, :], mod_ref[0, 4:5, :])
        h_ref[...] = h.astype(BF16)
        o_ref[...] = x

    h = h_ref[...]
    gate = _dot(h, wg_ref[0])
    up = _dot(h, wu_ref[0])
    prev = jnp.where(i % tiles_per_seq == 0, 0.0, halo_ref[j])
    row = lax.broadcasted_iota(jnp.int32, (tm, tf), 0)
    g1 = jnp.where(row == 0, prev[7:8, :], pltpu.roll(gate, 1, axis=0))
    g2 = jnp.where(row == 0, prev[6:7, :], jnp.where(row == 1, prev[7:8, :], pltpu.roll(gate, 2, axis=0)))
    halo_ref[j] = gate[tm - 8:tm, :]
    conv = cw_ref[0, 0:1, :] * g2 + cw_ref[0, 1:2, :] * g1 + cw_ref[0, 2:3, :] * gate + cb_ref[0]
    act = (_silu(conv) * up).astype(BF16)
    o_ref[...] += mod_ref[0, 5:6, :] * _dot(act, wd_ref[0])

    if final_norm:
        @pl.when(j == pl.num_programs(1) - 1)
        def _():
            y = o_ref[...]
            o_ref[...] = y * lax.rsqrt(jnp.mean(y * y, axis=-1, keepdims=True) + EPS) * fw_ref[...]


def _conv_ffn(x2, mod, norm_w, wg, wu, conv_w, conv_b, wd, final_w, l, T):
    M, D = x2.shape
    L, _, F = wg.shape
    tm, tf = 1024, 512
    tpb = T // tm
    return pl.pallas_call(
        functools.partial(_ffn_kernel, tiles_per_seq=tpb, final_norm=(l == L - 1)),
        grid=(M // tm, F // tf),
        in_specs=[pl.BlockSpec((tm, D), lambda i, j: (i, 0), pipeline_mode=pl.Buffered(1)),
                  pl.BlockSpec((1, 6, D), lambda i, j: (i // tpb, 0, 0)),
                  pl.BlockSpec((1, D), lambda i, j: (0, 0)),
                  pl.BlockSpec((1, D, tf), lambda i, j: (l, 0, j)),
                  pl.BlockSpec((1, D, tf), lambda i, j: (l, 0, j)),
                  pl.BlockSpec((1, FFN_CONV, tf), lambda i, j: (l, 0, j)),
                  pl.BlockSpec((1, 1, tf), lambda i, j: (l, 0, j)),
                  pl.BlockSpec((1, tf, D), lambda i, j: (l, j, 0)),
                  pl.BlockSpec((1, D), lambda i, j: (0, 0))],
        out_specs=pl.BlockSpec((tm, D), lambda i, j: (i, 0)),
        out_shape=jax.ShapeDtypeStruct((M, D), F32),
        scratch_shapes=[pltpu.VMEM((tm, D), BF16),
                        pltpu.VMEM((F // tf, 8, tf), F32)],
        compiler_params=pltpu.CompilerParams(dimension_semantics=("arbitrary", "arbitrary"),
                                             vmem_limit_bytes=FFN_VMEM_LIMIT),
        name="conv_ffn",
    )(x2, mod, norm_w.reshape(1, D), wg, wu, conv_w, conv_b.reshape(L, 1, F), wd, final_w.reshape(1, D))


def kernel(x, c, w_mod, b_mod, norm_mix, norm_ffn, w_in, gla_w_lr, gla_b_lr, gla_norm, gdn_conv, gdn_a_log, gdn_dt_bias, gdn_norm, pool_w, pool_scale, fox_f_bias, w_out, ffn_w_gate, ffn_w_up, ffn_conv_w, ffn_conv_b, ffn_w_down, norm_final):
    B, T, D = x.shape
    L = w_mod.shape[0]
    assert D == D_MODEL and T % 1024 == 0 and B <= 8
    consts = _chunk_constants()

    c_pad = jnp.zeros((8, D), F32).at[:B].set(c)
    mod_all = _modulation(c_pad, w_mod, b_mod)
    w_main, w_small = _prepare_in_weights(w_in)
    w_gate, w_up, w_down = ffn_w_gate.astype(BF16), ffn_w_up.astype(BF16), ffn_w_down.astype(BF16)
    x2 = x.reshape(B * T, D)
    for l in range(L):
        mod = mod_all[l, :B].reshape(B, 6, D)
        proj, small = _in_projection(x2, mod, norm_mix[l], w_main, w_small, l, T)
        y_a = _gla_mixer(proj, small, gla_w_lr[l], gla_b_lr[l], gla_norm[l], consts, B, T)
        y_b = _gdn_mixer(proj, small, gdn_conv[l], gdn_a_log[l], gdn_dt_bias[l], gdn_norm[l], consts, B, T)
        y_c = _pool_mixer(proj, pool_w[l].astype(BF16), pool_scale[l], B, T)
        y_d = _fox_attention(proj, _fox_gate_cumsum(small, fox_f_bias[l], B, T), B, T)
        x2 = _out_projection((y_a, y_b, y_c, y_d), w_out[l].astype(BF16), x2, mod, T)
        x2 = _conv_ffn(x2, mod, norm_ffn[l], w_gate, w_up, ffn_conv_w, ffn_conv_b, w_down, norm_final, l, T)
    return x2.reshape(B, T, D)
```

```python
import functools
import math

import numpy as np
import jax
import jax.numpy as jnp
from jax import lax
from jax.experimental import pallas as pl
from jax.experimental.pallas import tpu as pltpu

F32 = jnp.float32
BF16 = jnp.bfloat16
HIGHEST = lax.Precision.HIGHEST

D_MODEL = 2048
GROUP_WIDTH = D_MODEL // 4
N_HEADS = 4
HEAD_DIM = GROUP_WIDTH // 4
GLA_KEY_DIM = HEAD_DIM // 2
GLA_GATE_RANK = 16
GLA_GATE_TAU = 16.0
GDN_CONV = 4
POOL_WINDOWS = (2, 4, 8, 16)
CHUNK = 64
D_FF = 256 * int(math.ceil(8 * D_MODEL / 3 / 256))
FFN_CONV = 3
EPS = 1e-6

IN_SPLITS = (256, 256, 512, 512, 16, 1536, 512, 4, 4, 512, 512, 512, 512, 4)
IN_OFFS = tuple(int(sum(IN_SPLITS[:i])) for i in range(len(IN_SPLITS) + 1))

LANES = 128
P_FOX = 0
P_GLA = 1536
P_GDN = 3072
P_GDN_G = 4608
P_POOL = 5120
P_WIDTH = 5632
SM_GLA_LR = 0
SM_GDN_B = 16
SM_GDN_A = 20
SM_FOX_F = 28
FOX_ROW0 = SM_FOX_F // 8 * 8

VMEM_LIMIT = 56 * 1024 * 1024
FFN_VMEM_LIMIT = 60 * 1024 * 1024


def _cparams(n_axes):
    return pltpu.CompilerParams(dimension_semantics=("arbitrary",) * n_axes,
                                vmem_limit_bytes=VMEM_LIMIT)


def _dot(a, b, precision=None):
    return jnp.dot(a, b, preferred_element_type=F32, precision=precision)


def _dot_nt(a, b):
    return lax.dot_general(a, b, (((1,), (1,)), ((), ())), preferred_element_type=F32)


def _bdot(a, b):
    return lax.dot_general(a, b, (((2,), (1,)), ((0,), (0,))), preferred_element_type=F32)


def _bdot_nt(a, b):
    return lax.dot_general(a, b, (((2,), (2,)), ((0,), (0,))), preferred_element_type=F32)


def _dot_sel(sel_bf16, x):
    x1 = x.astype(BF16)
    r1 = x - x1.astype(F32)
    x2 = r1.astype(BF16)
    x3 = (r1 - x2.astype(F32)).astype(BF16)
    return _dot(sel_bf16, x1) + _dot(sel_bf16, x2) + _dot(sel_bf16, x3)


def _bdot_sel(sel_bf16, x):
    x1 = x.astype(BF16)
    r1 = x - x1.astype(F32)
    x2 = r1.astype(BF16)
    x3 = (r1 - x2.astype(F32)).astype(BF16)
    return _bdot(sel_bf16, x1) + _bdot(sel_bf16, x2) + _bdot(sel_bf16, x3)


def _log_sigmoid(x):
    return jnp.minimum(x, 0.0) - jnp.log1p(jnp.exp(-jnp.abs(x)))


def _softplus(x):
    return jnp.maximum(x, 0.0) + jnp.log1p(jnp.exp(-jnp.abs(x)))


def _silu(x):
    return x * jax.nn.sigmoid(x)


def _rms_mod(x, norm_w, shift, scale):
    y = x * lax.rsqrt(jnp.mean(x * x, axis=-1, keepdims=True) + EPS) * norm_w
    return y * (1.0 + scale) + shift


def _mod_kernel(c_ref, w_ref, b_ref, o_ref):
    cond = _silu(c_ref[...])
    o_ref[0] = _dot(cond, w_ref[0], precision=HIGHEST) + b_ref[0]


def _modulation(c_pad, w_mod, b_mod):
    L, D, N = w_mod.shape
    tn = 1024
    return pl.pallas_call(
        _mod_kernel,
        grid=(L, N // tn),
        in_specs=[pl.BlockSpec((8, D), lambda l, j: (0, 0)),
                  pl.BlockSpec((1, D, tn), lambda l, j: (l, 0, j)),
                  pl.BlockSpec((1, 1, tn), lambda l, j: (l, 0, j))],
        out_specs=pl.BlockSpec((1, 8, tn), lambda l, j: (l, 0, j)),
        out_shape=jax.ShapeDtypeStruct((L, 8, N), F32),
        compiler_params=_cparams(2),
        name="modulation",
    )(c_pad, w_mod, b_mod.reshape(L, 1, N))


def _wprep_kernel(w_ref, main_ref, small_ref):
    tk = w_ref.shape[2]
    nf = IN_SPLITS[13]
    assert (SM_GLA_LR, SM_GDN_B, SM_GDN_A, SM_FOX_F) == (0, 16, 20, 28) and nf == 4
    for l in range(w_ref.shape[1]):
        off = 0
        for i in (10, 11, 12, 0, 1, 2, 3, 5, 6, 9):
            width = IN_SPLITS[i]
            main_ref[l, :, off:off + width] = jnp.transpose(w_ref[IN_OFFS[i]:IN_OFFS[i + 1], l, :]).astype(BF16)
            off += width
        tail = w_ref[IN_OFFS[14] - 8:IN_OFFS[14], l, :]
        tail = jnp.where(lax.broadcasted_iota(jnp.int32, (8, tk), 0) >= 8 - nf, tail, 0.0)
        narrow = jnp.concatenate(
            [w_ref[IN_OFFS[4]:IN_OFFS[5], l, :],
             w_ref[IN_OFFS[7]:IN_OFFS[9], l, :],
             tail,
             jnp.zeros((LANES - 32, tk), F32)], axis=0)
        small_ref[l] = jnp.transpose(narrow).astype(BF16)


def _prepare_in_weights(w_in):
    L, D, N = w_in.shape
    tk = 128
    wt = jnp.transpose(w_in, (2, 0, 1))
    return pl.pallas_call(
        _wprep_kernel,
        grid=(D // tk,),
        in_specs=[pl.BlockSpec((N, L, tk), lambda r: (0, 0, r))],
        out_specs=[pl.BlockSpec((L, tk, P_WIDTH), lambda r: (0, r, 0)),
                   pl.BlockSpec((L, tk, LANES), lambda r: (0, r, 0))],
        out_shape=[jax.ShapeDtypeStruct((L, D, P_WIDTH), BF16),
                   jax.ShapeDtypeStruct((L, D, LANES), BF16)],
        compiler_params=_cparams(1),
        name="in_weight_prep",
    )(wt)


def _inproj_kernel(x_ref, mod_ref, nw_ref, w_ref, ws_ref, o_ref, os_ref, h_ref):
    @pl.when(pl.program_id(1) == 0)
    def _():
        h = _rms_mod(x_ref[...], nw_ref[...], mod_ref[0, 0:1, :], mod_ref[0, 1:2, :])
        h_ref[...] = h.astype(BF16)
        os_ref[...] = _dot(h_ref[...], ws_ref[0])

    o_ref[...] = _dot(h_ref[...], w_ref[0])


def _in_projection(x2, mod, norm_w, w_main, w_small, l, T):
    M, D = x2.shape
    N = w_main.shape[2]
    tm, tn = 1024, 512
    tpb = T // tm
    return pl.pallas_call(
        _inproj_kernel,
        grid=(M // tm, N // tn),
        in_specs=[pl.BlockSpec((tm, D), lambda i, j: (i, 0)),
                  pl.BlockSpec((1, 6, D), lambda i, j: (i // tpb, 0, 0)),
                  pl.BlockSpec((1, D), lambda i, j: (0, 0)),
                  pl.BlockSpec((1, D, tn), lambda i, j: (l, 0, j)),
                  pl.BlockSpec((1, D, LANES), lambda i, j: (l, 0, 0))],
        out_specs=[pl.BlockSpec((tm, tn), lambda i, j: (i, j)),
                   pl.BlockSpec((tm, LANES), lambda i, j: (i, 0))],
        out_shape=[jax.ShapeDtypeStruct((M, N), F32),
                   jax.ShapeDtypeStruct((M, LANES), F32)],
        scratch_shapes=[pltpu.VMEM((tm, D), BF16)],
        compiler_params=_cparams(2),
        name="in_projection",
    )(x2, mod, norm_w.reshape(1, D), w_main, w_small)


GLA_LEVELS = 7
GLA_GROUP = 4
GDN_GROUP = 4
GDN_PAIR = 2
GDN_LEVELS = 6


def _chunk_constants():
    i = np.arange(CHUNK)[:, None]
    j = np.arange(CHUNK)[None, :]
    tri = (i >= j).astype(np.float32)
    strict = (i > j).astype(np.float32)
    sels, masks = [], [(i == j).astype(np.float32)]
    for lv in range(1, GLA_LEVELS):
        m = 1 << (lv - 1)
        ref_row = (np.arange(CHUNK) // (2 * m)) * 2 * m + m
        sels.append((j == ref_row[:, None]).astype(np.float32))
        same = (i // (2 * m)) == (j // (2 * m))
        masks.append((same & ((i % (2 * m)) >= m) & ((j % (2 * m)) < m)).astype(np.float32))
    sel_all = np.concatenate(sels, axis=0)
    return tri, strict, sel_all, np.stack(masks)


def _gla_kernel(p_ref, sm_ref, wlr_ref, blr_ref, nw_ref, tri_ref, sel_ref, msk_ref, bd_ref, o_ref, s_ref):
    tb = p_ref.shape[0]
    dk, dv = GLA_KEY_DIM, HEAD_DIM
    kw = N_HEADS * dk
    G = GLA_GROUP
    R = G * CHUNK
    NP = N_HEADS // 2
    NB = NP * G
    pw = 2 * dk
    vw = 2 * dv

    @pl.when(pl.program_id(1) == 0)
    def _():
        s_ref[...] = jnp.zeros_like(s_ref)

    def to_problems(a, w):
        return jnp.concatenate([a[:, p * w:(p + 1) * w].reshape(G, CHUNK, w) for p in range(NP)], axis=0)

    def pair_block_diag(a, w):
        lane = lax.broadcasted_iota(jnp.int32, a.shape, 2)
        zero = jnp.zeros_like(a)
        return jnp.concatenate([jnp.where(lane < w, a, zero), jnp.where(lane >= w, a, zero)], axis=1)

    def group(i, carry):
        r0 = pl.multiple_of(i * R, R)
        rows = pl.ds(r0, R)
        lr = sm_ref[rows, SM_GLA_LR:SM_GLA_LR + GLA_GATE_RANK]
        logits = _dot(lr, wlr_ref[...], precision=HIGHEST) + blr_ref[...]
        logg = (_log_sigmoid(logits) * (1.0 / GLA_GATE_TAU)).reshape(G, CHUNK, kw)
        tri = jnp.broadcast_to(tri_ref[...][None], (G, CHUNK, CHUNK))
        b3 = _bdot_sel(tri, logg)
        sel = jnp.broadcast_to(sel_ref[...][None], (G,) + sel_ref.shape)
        refs = _bdot_sel(sel, b3)
        b = b3.reshape(R, kw)
        q = p_ref[rows, 0:kw] * dk ** -0.5
        k = p_ref[rows, kw:2 * kw]
        att = None
        for lv in range(GLA_LEVELS):
            if lv == 0:
                qt, kt = q, k
            else:
                r = refs[:, (lv - 1) * CHUNK:lv * CHUNK, :].reshape(R, kw)
                qt = q * jnp.exp(jnp.minimum(b - r, 0.0))
                kt = k * jnp.exp(jnp.minimum(r - b, 0.0))
            qp = to_problems(qt.astype(BF16), pw)
            kp = pair_block_diag(to_problems(kt.astype(BF16), pw), dk)
            a = _bdot_nt(qp, kp) * msk_ref[lv]
            att = a if att is None else att + a
        v = p_ref[rows, 2 * kw:2 * kw + N_HEADS * dv]
        vp = to_problems(v.astype(BF16), vw)
        o = _bdot(att.astype(BF16), pair_block_diag(vp, dv))

        blast = jnp.broadcast_to(b3[:, CHUNK - 1:CHUNK, :], (G, CHUNK, kw)).reshape(R, kw)
        qe = to_problems((q * jnp.exp(b)).astype(BF16), pw)
        kl = to_problems(k * jnp.exp(blast - b), pw)
        dcol = to_problems(jnp.exp(blast), pw)
        bd = bd_ref[...]
        for p in range(NP):
            st = s_ref[p]
            for g in range(G):
                n = p * G + g
                o_n = o[n] + _dot(qe[n], st.astype(BF16))
                upd = _dot(jnp.transpose(kl[n]).astype(BF16), vp[n]) * bd
                st = st * jnp.transpose(dcol[n])[:, 0:1] + upd
                for j in range(2):
                    h = 2 * p + j
                    oh = o_n[:, j * dv:(j + 1) * dv]
                    on = oh * lax.rsqrt(jnp.mean(oh * oh, axis=-1, keepdims=True) + EPS) * nw_ref[...]
                    gate = p_ref[pl.ds(r0 + g * CHUNK, CHUNK), 2 * kw + (N_HEADS + h) * dv:2 * kw + (N_HEADS + h + 1) * dv]
                    o_ref[pl.ds(r0 + g * CHUNK, CHUNK), h * dv:(h + 1) * dv] = (on * _silu(gate)).astype(BF16)
            s_ref[p] = st
        return carry

    lax.fori_loop(0, tb // R, group, 0)


def _gla_mixer(proj, small, w_lr, b_lr, norm_w, consts, B, T):
    tri, _, sel_all, masks = consts
    M = proj.shape[0]
    tb = 512
    nt = T // tb
    kw = N_HEADS * GLA_KEY_DIM
    masks2 = np.concatenate([masks, masks], axis=2)
    bd = np.kron(np.eye(2, dtype=np.float32), np.ones((GLA_KEY_DIM, HEAD_DIM), np.float32))
    const = lambda a: pl.BlockSpec(a.shape, lambda b, t: (0,) * a.ndim)
    return pl.pallas_call(
        _gla_kernel,
        grid=(B, nt),
        in_specs=[pl.BlockSpec((tb, 1536), lambda b, t: (b * nt + t, P_GLA // 1536)),
                  pl.BlockSpec((tb, LANES), lambda b, t: (b * nt + t, 0)),
                  pl.BlockSpec((GLA_GATE_RANK, kw), lambda b, t: (0, 0)),
                  pl.BlockSpec((1, kw), lambda b, t: (0, 0)),
                  pl.BlockSpec((1, HEAD_DIM), lambda b, t: (0, 0)),
                  const(tri), const(sel_all), const(masks2), const(bd)],
        out_specs=pl.BlockSpec((tb, GROUP_WIDTH), lambda b, t: (b * nt + t, 0)),
        out_shape=jax.ShapeDtypeStruct((M, GROUP_WIDTH), BF16),
        scratch_shapes=[pltpu.VMEM((N_HEADS // 2, 2 * GLA_KEY_DIM, 2 * HEAD_DIM), F32)],
        compiler_params=_cparams(2),
        name="gla_mixer",
    )(proj, small, w_lr, b_lr.reshape(1, kw), norm_w.reshape(1, HEAD_DIM),
      jnp.asarray(tri, BF16), jnp.asarray(sel_all, BF16), jnp.asarray(masks2, F32), jnp.asarray(bd, F32))


def _gdn_kernel(x_ref, g_ref, sm_ref, cw_ref, alog_ref, dtb_ref, nw_ref, tri_ref, incl_ref, lvl_ref,
                o_ref, s_ref, halo_ref, cq_ref, ku_ref, au_ref, gl_ref):
    tb = x_ref.shape[0]
    d = HEAD_DIM
    hw = N_HEADS * d
    G = GDN_GROUP
    R = G * CHUNK
    P = GDN_PAIR
    BR = P * CHUNK
    NP = N_HEADS // P

    @pl.when(pl.program_id(1) == 0)
    def _():
        s_ref[...] = jnp.zeros_like(s_ref)
        halo_ref[...] = jnp.zeros_like(halo_ref)

    def stack(a):
        w = a.shape[1] // N_HEADS
        head = lambda h: a[:, h * w:(h + 1) * w].reshape(G, CHUNK, w)
        return jnp.concatenate([jnp.concatenate([head(p * P + j) for j in range(P)], axis=1)
                                for p in range(NP)], axis=0)

    def gate_col(a, lane, last=False):
        def head(h):
            col = a[:, lane + h:lane + h + 1].reshape(G, CHUNK, 1)
            return jnp.broadcast_to(col[:, CHUNK - 1:CHUNK, :], (G, CHUNK, 1)) if last else col
        return jnp.concatenate([jnp.concatenate([head(p * P + j) for j in range(P)], axis=1)
                                for p in range(NP)], axis=0)

    incl = incl_ref[...]
    head_of_row = lax.broadcasted_iota(jnp.int32, (BR, d), 0) // CHUNK

    def block_diag(a):
        return jnp.concatenate([jnp.where(head_of_row == j, a, 0.0) for j in range(P)], axis=2)

    def prepare(i, carry):
        r0 = pl.multiple_of(i * R, R)
        rows = pl.ds(r0, R)
        prev = x_ref[pl.ds(pl.multiple_of(jnp.maximum(r0 - 8, 0), 8), 8), :]
        prev = jnp.where(i == 0, halo_ref[...], prev)
        win = jnp.concatenate([prev, x_ref[rows, :]], axis=0)
        conv = win * cw_ref[GDN_CONV - 1:GDN_CONV, :]
        for s in range(1, GDN_CONV):
            conv = conv + pltpu.roll(win, s, axis=0) * cw_ref[GDN_CONV - 1 - s:GDN_CONV - s, :]
        qkv = _silu(conv[8:, :])

        sm = sm_ref[rows, :]
        beta = jax.nn.sigmoid(sm)
        g = -jnp.exp(alog_ref[...]) * _softplus(sm + dtb_ref[...])
        gcum = _dot_sel(tri_ref[...], g)
        for j in range(G):
            gl_ref[i * G + j] = jnp.broadcast_to(jnp.exp(gcum[(j + 1) * CHUNK - 1:(j + 1) * CHUNK, :]), (8, LANES))

        qs, ks, vs = stack(qkv[:, 0:hw]), stack(qkv[:, hw:2 * hw]), stack(qkv[:, 2 * hw:3 * hw])
        qn = qs * lax.rsqrt(jnp.sum(qs * qs, axis=-1, keepdims=True) + EPS) * d ** -0.5
        kn = ks * lax.rsqrt(jnp.sum(ks * ks, axis=-1, keepdims=True) + EPS)
        bcol = gate_col(beta, SM_GDN_B)
        gcol = gate_col(gcum, SM_GDN_A)
        gmat = jnp.broadcast_to(gcol, (NP * G, BR, BR))
        grow = jnp.stack([jnp.transpose(gmat[b]) for b in range(NP * G)])
        decay = jnp.exp(jnp.minimum(gcol - grow, 0.0)) * incl
        kb = kn * bcol
        kn16 = kn.astype(BF16)
        lmat = _bdot_nt(kb.astype(BF16), kn16) * decay
        tinv = lvl_ref[GDN_LEVELS] - lmat * lvl_ref[0]
        for lv in range(1, GDN_LEVELS):
            t16 = tinv.astype(BF16)
            tl = _bdot(t16, (lmat * lvl_ref[lv]).astype(BF16))
            tinv = tinv - _bdot(tl.astype(BF16), t16)
        rhs = jnp.concatenate([vs * bcol, kb * jnp.exp(gcol)], axis=2)
        sol = _bdot(tinv.astype(BF16), rhs.astype(BF16))
        sol16 = sol.astype(BF16)
        attn = _bdot_nt(qn.astype(BF16), kn16) * decay
        aw = _bdot(attn.astype(BF16), sol16)
        qp = (qn * jnp.exp(gcol) - aw[:, :, d:]).astype(BF16)
        kdec = kn * jnp.exp(gate_col(gcum, SM_GDN_A, last=True) - gcol)
        kd = block_diag(kdec)
        for p in range(NP):
            for g_ in range(G):
                b, c = p * G + g_, i * G + g_
                kt = jnp.transpose(kd[b]).astype(BF16)
                ku = _dot(kt, sol16[b])
                for j in range(P):
                    h = p * P + j
                    ku_ref[c, h * d:(h + 1) * d, :] = ku[j * d:(j + 1) * d, :d]
                    au_ref[c, h * CHUNK:(h + 1) * CHUNK, :] = aw[b, j * CHUNK:(j + 1) * CHUNK, :d]
                    cq_ref[c * N_HEADS + h, 0:d, :] = ku[j * d:(j + 1) * d, d:].astype(BF16)
                    cq_ref[c * N_HEADS + h, d:d + CHUNK, :] = qp[b, j * CHUNK:(j + 1) * CHUNK, :]
        return carry

    lax.fori_loop(0, tb // R, prepare, 0)
    halo_ref[...] = x_ref[tb - 8:tb, :]

    def recur(c, carry):
        rows = pl.ds(pl.multiple_of(c * CHUNK, CHUNK), CHUNK)
        gl = gl_ref[c]
        for h in range(N_HEADS):
            hs = slice(h * d, (h + 1) * d)
            st = s_ref[h]
            r = _dot(cq_ref[c * N_HEADS + h], st.astype(BF16))
            s_ref[h] = st * gl[0:1, SM_GDN_A + h:SM_GDN_A + h + 1] - r[:d] + ku_ref[c, hs, :]
            oh = r[d:] + au_ref[c, h * CHUNK:(h + 1) * CHUNK, :]
            on = oh * lax.rsqrt(jnp.mean(oh * oh, axis=-1, keepdims=True) + EPS) * nw_ref[...]
            o_ref[rows, hs] = (on * _silu(g_ref[rows, hs])).astype(BF16)
        return carry

    lax.fori_loop(0, tb // CHUNK, recur, 0)


def _gdn_mixer(proj, small, conv_w, a_log, dt_bias, norm_w, consts, B, T):
    tri, _, _, masks = consts
    M = proj.shape[0]
    tb = 512
    nt = T // tb
    nc = tb // CHUNK
    hb = N_HEADS * CHUNK
    pad = lambda v: jnp.zeros((1, LANES), F32).at[0, SM_GDN_A:SM_GDN_A + N_HEADS].set(v)
    tri_g = np.kron(np.eye(GDN_GROUP, dtype=np.float32), tri)
    incl_bd = np.kron(np.eye(GDN_PAIR, dtype=np.float32), tri)
    eye_p = np.eye(GDN_PAIR, dtype=np.float32)
    lvl = np.stack([np.kron(eye_p, m) for m in masks[1:]] + [np.eye(GDN_PAIR * CHUNK, dtype=np.float32)])
    assert lvl.shape[0] == GDN_LEVELS + 1
    const = lambda a: pl.BlockSpec(a.shape, lambda b, t: (0,) * a.ndim)
    return pl.pallas_call(
        _gdn_kernel,
        grid=(B, nt),
        in_specs=[pl.BlockSpec((tb, 1536), lambda b, t: (b * nt + t, P_GDN // 1536)),
                  pl.BlockSpec((tb, GROUP_WIDTH), lambda b, t: (b * nt + t, P_GDN_G // GROUP_WIDTH)),
                  pl.BlockSpec((tb, LANES), lambda b, t: (b * nt + t, 0)),
                  pl.BlockSpec((GDN_CONV, 1536), lambda b, t: (0, 0)),
                  pl.BlockSpec((1, LANES), lambda b, t: (0, 0)),
                  pl.BlockSpec((1, LANES), lambda b, t: (0, 0)),
                  pl.BlockSpec((1, HEAD_DIM), lambda b, t: (0, 0)),
                  const(tri_g), const(incl_bd), const(lvl)],
        out_specs=pl.BlockSpec((tb, GROUP_WIDTH), lambda b, t: (b * nt + t, 0)),
        out_shape=jax.ShapeDtypeStruct((M, GROUP_WIDTH), BF16),
        scratch_shapes=[pltpu.VMEM((N_HEADS, HEAD_DIM, HEAD_DIM), F32),
                        pltpu.VMEM((8, 1536), F32),
                        pltpu.VMEM((nc * N_HEADS, HEAD_DIM + CHUNK, HEAD_DIM), BF16),
                        pltpu.VMEM((nc, N_HEADS * HEAD_DIM, HEAD_DIM), F32),
                        pltpu.VMEM((nc, hb, HEAD_DIM), F32),
                        pltpu.VMEM((nc, 8, LANES), F32)],
        compiler_params=_cparams(2),
        name="gdn_mixer",
    )(proj, proj, small, conv_w, pad(a_log), pad(dt_bias), norm_w.reshape(1, HEAD_DIM),
      jnp.asarray(tri_g, BF16), jnp.asarray(incl_bd, F32), jnp.asarray(lvl, F32))


POOL_HALO = 16


def _pool_kernel(u_ref, w_ref, sc_ref, o_ref, halo_ref):
    tb = u_ref.shape[0]
    gd = GROUP_WIDTH // len(POOL_WINDOWS)
    t = pl.program_id(1)

    @pl.when(t == 0)
    def _():
        halo_ref[...] = jnp.zeros_like(halo_ref)

    pos = (t * tb + 1 + lax.broadcasted_iota(jnp.int32, (tb, 1), 0)).astype(F32)
    for gi, win in enumerate(POOL_WINDOWS):
        cols = slice(gi * gd, (gi + 1) * gd)
        u = u_ref[:, cols]
        ssum = jnp.concatenate([halo_ref[:, cols], u], axis=0)
        shift = 1
        while shift < win:
            ssum = ssum + pltpu.roll(ssum, shift, axis=0)
            shift *= 2
        mean = ssum[POOL_HALO:, :] / jnp.minimum(pos, float(win))
        y = _dot((mean - u).astype(BF16), w_ref[gi])
        o_ref[:, cols] = (y * sc_ref[:, cols]).astype(BF16)
    halo_ref[...] = u_ref[tb - POOL_HALO:tb, :]


def _pool_mixer(proj, w_bf16, scale, B, T):
    M = proj.shape[0]
    tb = 512
    nt = T // tb
    gd = GROUP_WIDTH // len(POOL_WINDOWS)
    return pl.pallas_call(
        _pool_kernel,
        grid=(B, nt),
        in_specs=[pl.BlockSpec((tb, GROUP_WIDTH), lambda b, t: (b * nt + t, P_POOL // GROUP_WIDTH)),
                  pl.BlockSpec((len(POOL_WINDOWS), gd, gd), lambda b, t: (0, 0, 0)),
                  pl.BlockSpec((1, GROUP_WIDTH), lambda b, t: (0, 0))],
        out_specs=pl.BlockSpec((tb, GROUP_WIDTH), lambda b, t: (b * nt + t, 0)),
        out_shape=jax.ShapeDtypeStruct((M, GROUP_WIDTH), BF16),
        scratch_shapes=[pltpu.VMEM((POOL_HALO, GROUP_WIDTH), F32)],
        compiler_params=_cparams(2),
        name="pool_mixer",
    )(proj, w_bf16, scale.reshape(1, GROUP_WIDTH))


FOX_CUM_BLOCK = 512


def _fox_gate_kernel(sm_ref, bias_ref, tri_ref, o_ref, carry_ref):
    @pl.when(pl.program_id(1) == 0)
    def _():
        carry_ref[...] = jnp.zeros_like(carry_ref)

    lf = _log_sigmoid(sm_ref[...] + bias_ref[...])
    cum = _dot_sel(tri_ref[...], lf) + carry_ref[...]
    carry_ref[...] = cum[FOX_CUM_BLOCK - 1:FOX_CUM_BLOCK, :]
    o_ref[0] = jnp.transpose(cum)[FOX_ROW0:FOX_ROW0 + 8, :]


def _fox_gate_cumsum(small, f_bias, B, T):
    tb = FOX_CUM_BLOCK
    nt = T // tb
    tri = jnp.asarray(np.tril(np.ones((tb, tb), np.float32)), BF16)
    bias = jnp.zeros((1, LANES), F32).at[0, SM_FOX_F:SM_FOX_F + N_HEADS].set(f_bias)
    return pl.pallas_call(
        _fox_gate_kernel,
        grid=(B, nt),
        in_specs=[pl.BlockSpec((tb, LANES), lambda b, t: (b * nt + t, 0)),
                  pl.BlockSpec((1, LANES), lambda b, t: (0, 0)),
                  pl.BlockSpec((tb, tb), lambda b, t: (0, 0))],
        out_specs=pl.BlockSpec((1, 8, tb), lambda b, t: (b, 0, t)),
        out_shape=jax.ShapeDtypeStruct((B, 8, T), F32),
        scratch_shapes=[pltpu.VMEM((1, LANES), F32)],
        compiler_params=_cparams(2),
        name="fox_gate_cumsum",
    )(small, bias, tri)


FOX_BLOCK = 512


LOG2E = math.log2(math.e)


def _fox_kernel(q_ref, k_ref, v_ref, f_ref, o_ref, kb_ref, vb_ref):
    qi = pl.program_id(1)
    blk = FOX_BLOCK
    d = HEAD_DIM

    @pl.when(qi == 0)
    def _():
        kb_ref[...] = k_ref[...].astype(BF16)
        vb_ref[...] = v_ref[...].astype(BF16)

    q = (q_ref[...] * (d ** -0.5 * LOG2E)).astype(BF16)
    q0 = pl.multiple_of(qi * blk, blk)
    fh = lambda h, start, n: f_ref[0, SM_FOX_F - FOX_ROW0 + h:SM_FOX_F - FOX_ROW0 + h + 1, pl.ds(start, n)]
    f0 = [fh(h, q0, LANES)[:, 0:1] for h in range(N_HEADS)]

    def block(ki, carry, masked):
        k0 = pl.multiple_of(ki * blk, blk)
        out = []
        for h in range(N_HEADS):
            m, l, acc = carry[h]
            hs = slice(h * d, (h + 1) * d)
            fk = (fh(h, k0, blk) - f0[h]) * LOG2E
            s = _dot_nt(q[:, hs], kb_ref[pl.ds(k0, blk), hs]) - fk
            if masked:
                row = lax.broadcasted_iota(jnp.int32, (blk, blk), 0)
                col = lax.broadcasted_iota(jnp.int32, (blk, blk), 1)
                s = jnp.where(row >= col, s, -jnp.inf)
            m_new = jnp.maximum(m, jnp.max(s, axis=-1, keepdims=True))
            alpha = jnp.exp2(m - m_new)
            p = jnp.exp2(s - m_new)
            l = alpha * l + jnp.sum(p, axis=-1, keepdims=True)
            acc = alpha * acc + _dot(p.astype(BF16), vb_ref[pl.ds(k0, blk), hs])
            out.append((m_new, l, acc))
        return tuple(out)

    init = tuple((jnp.full((blk, 1), -jnp.inf, F32), jnp.zeros((blk, 1), F32), jnp.zeros((blk, d), F32))
                 for _ in range(N_HEADS))
    carry = lax.fori_loop(0, qi, lambda ki, c: block(ki, c, False), init)
    carry = block(qi, carry, True)
    for h in range(N_HEADS):
        _, l, acc = carry[h]
        o_ref[:, h * d:(h + 1) * d] = (acc / l).astype(BF16)


def _fox_attention(proj, fcum, B, T):
    M = proj.shape[0]
    blk = FOX_BLOCK
    nq = T // blk
    return pl.pallas_call(
        _fox_kernel,
        grid=(B, nq),
        in_specs=[pl.BlockSpec((blk, GROUP_WIDTH), lambda b, i: (b * nq + i, P_FOX // GROUP_WIDTH)),
                  pl.BlockSpec((T, GROUP_WIDTH), lambda b, i: (b, P_FOX // GROUP_WIDTH + 1)),
                  pl.BlockSpec((T, GROUP_WIDTH), lambda b, i: (b, P_FOX // GROUP_WIDTH + 2)),
                  pl.BlockSpec((1, 8, T), lambda b, i: (b, 0, 0))],
        out_specs=pl.BlockSpec((blk, GROUP_WIDTH), lambda b, i: (b * nq + i, 0)),
        out_shape=jax.ShapeDtypeStruct((M, GROUP_WIDTH), BF16),
        scratch_shapes=[pltpu.VMEM((T, GROUP_WIDTH), BF16), pltpu.VMEM((T, GROUP_WIDTH), BF16)],
        compiler_params=_cparams(2),
        name="fox_attention",
    )(proj, proj, proj, fcum)


def _outproj_kernel(ya_ref, yb_ref, yc_ref, yd_ref, w_ref, x_ref, mod_ref, o_ref):
    gw = GROUP_WIDTH
    acc = _dot(ya_ref[...], w_ref[0:gw, :])
    acc = acc + _dot(yb_ref[...], w_ref[gw:2 * gw, :])
    acc = acc + _dot(yc_ref[...], w_ref[2 * gw:3 * gw, :])
    acc = acc + _dot(yd_ref[...], w_ref[3 * gw:4 * gw, :])
    o_ref[...] = x_ref[...] + mod_ref[0, 2:3, :] * acc


def _out_projection(ys, w_bf16, x2, mod, T):
    M, D = x2.shape
    tm, tn = 1024, 1024
    tpb = T // tm
    yspec = pl.BlockSpec((tm, GROUP_WIDTH), lambda i, j: (i, 0))
    return pl.pallas_call(
        _outproj_kernel,
        grid=(M // tm, D // tn),
        in_specs=[yspec, yspec, yspec, yspec,
                  pl.BlockSpec((D, tn), lambda i, j: (0, j)),
                  pl.BlockSpec((tm, tn), lambda i, j: (i, j)),
                  pl.BlockSpec((1, 6, tn), lambda i, j: (i // tpb, 0, j))],
        out_specs=pl.BlockSpec((tm, tn), lambda i, j: (i, j)),
        out_shape=jax.ShapeDtypeStruct((M, D), F32),
        compiler_params=_cparams(2),
        name="out_projection",
    )(*ys, w_bf16, x2, mod)


def _ffn_kernel(x_ref, mod_ref, nw_ref, wg_ref, wu_ref, cw_ref, cb_ref, wd_ref, fw_ref, o_ref,
                h_ref, halo_ref, *, tiles_per_seq, final_norm):
    i = pl.program_id(0)
    j = pl.program_id(1)
    tm, tf = x_ref.shape[0], wg_ref.shape[2]

    @pl.when(j == 0)
    def _():
        x = x_ref[...]
        h = _rms_mod(x, nw_ref[...], mod_ref[0, 3:4, :], mod_ref[0, 4:5, :])
        h_ref[...] = h.astype(BF16)
        o_ref[...] = x

    h = h_ref[...]
    gate = _dot(h, wg_ref[0])
    up = _dot(h, wu_ref[0])
    prev = jnp.where(i % tiles_per_seq == 0, 0.0, halo_ref[j])
    row = lax.broadcasted_iota(jnp.int32, (tm, tf), 0)
    g1 = jnp.where(row == 0, prev[7:8, :], pltpu.roll(gate, 1, axis=0))
    g2 = jnp.where(row == 0, prev[6:7, :], jnp.where(row == 1, prev[7:8, :], pltpu.roll(gate, 2, axis=0)))
    halo_ref[j] = gate[tm - 8:tm, :]
    conv = cw_ref[0, 0:1, :] * g2 + cw_ref[0, 1:2, :] * g1 + cw_ref[0, 2:3, :] * gate + cb_ref[0]
    act = (_silu(conv) * up).astype(BF16)
    o_ref[...] += mod_ref[0, 5:6, :] * _dot(act, wd_ref[0])

    if final_norm:
        @pl.when(j == pl.num_programs(1) - 1)
        def _():
            y = o_ref[...]
            o_ref[...] = y * lax.rsqrt(jnp.mean(y * y, axis=-1, keepdims=True) + EPS) * fw_ref[...]


def _conv_ffn(x2, mod, norm_w, wg, wu, conv_w, conv_b, wd, final_w, l, T):
    M, D = x2.shape
    L, _, F = wg.shape
    tm, tf = 1024, 512
    tpb = T // tm
    return pl.pallas_call(
        functools.partial(_ffn_kernel, tiles_per_seq=tpb, final_norm=(l == L - 1)),
        grid=(M // tm, F // tf),
        in_specs=[pl.BlockSpec((tm, D), lambda i, j: (i, 0), pipeline_mode=pl.Buffered(1)),
                  pl.BlockSpec((1, 6, D), lambda i, j: (i // tpb, 0, 0)),
                  pl.BlockSpec((1, D), lambda i, j: (0, 0)),
                  pl.BlockSpec((1, D, tf), lambda i, j: (l, 0, j)),
                  pl.BlockSpec((1, D, tf), lambda i, j: (l, 0, j)),
                  pl.BlockSpec((1, FFN_CONV, tf), lambda i, j: (l, 0, j)),
                  pl.BlockSpec((1, 1, tf), lambda i, j: (l, 0, j)),
                  pl.BlockSpec((1, tf, D), lambda i, j: (l, j, 0)),
                  pl.BlockSpec((1, D), lambda i, j: (0, 0))],
        out_specs=pl.BlockSpec((tm, D), lambda i, j: (i, 0)),
        out_shape=jax.ShapeDtypeStruct((M, D), F32),
        scratch_shapes=[pltpu.VMEM((tm, D), BF16),
                        pltpu.VMEM((F // tf, 8, tf), F32)],
        compiler_params=pltpu.CompilerParams(dimension_semantics=("arbitrary", "arbitrary"),
                                             vmem_limit_bytes=FFN_VMEM_LIMIT),
        name="conv_ffn",
    )(x2, mod, norm_w.reshape(1, D), wg, wu, conv_w, conv_b.reshape(L, 1, F), wd, final_w.reshape(1, D))


def kernel(x, c, w_mod, b_mod, norm_mix, norm_ffn, w_in, gla_w_lr, gla_b_lr, gla_norm, gdn_conv, gdn_a_log, gdn_dt_bias, gdn_norm, pool_w, pool_scale, fox_f_bias, w_out, ffn_w_gate, ffn_w_up, ffn_conv_w, ffn_conv_b, ffn_w_down, norm_final):
    B, T, D = x.shape
    L = w_mod.shape[0]
    assert D == D_MODEL and T % 1024 == 0 and B <= 8
    consts = _chunk_constants()

    c_pad = jnp.zeros((8, D), F32).at[:B].set(c)
    mod_all = _modulation(c_pad, w_mod, b_mod)
    w_main, w_small = _prepare_in_weights(w_in)
    w_gate, w_up, w_down = ffn_w_gate.astype(BF16), ffn_w_up.astype(BF16), ffn_w_down.astype(BF16)
    x2 = x.reshape(B * T, D)
    for l in range(L):
        mod = mod_all[l, :B].reshape(B, 6, D)
        proj, small = _in_projection(x2, mod, norm_mix[l], w_main, w_small, l, T)
        y_a = _gla_mixer(proj, small, gla_w_lr[l], gla_b_lr[l], gla_norm[l], consts, B, T)
        y_b = _gdn_mixer(proj, small, gdn_conv[l], gdn_a_log[l], gdn_dt_bias[l], gdn_norm[l], consts, B, T)
        y_c = _pool_mixer(proj, pool_w[l].astype(BF16), pool_scale[l], B, T)
        y_d = _fox_attention(proj, _fox_gate_cumsum(small, fox_f_bias[l], B, T), B, T)
        x2 = _out_projection((y_a, y_b, y_c, y_d), w_out[l].astype(BF16), x2, mod, T)
        x2 = _conv_ffn(x2, mod, norm_ffn[l], w_gate, w_up, ffn_conv_w, ffn_conv_b, w_down, norm_final, l, T)
    return x2.reshape(B, T, D)
```

```python
import functools
import math

import numpy as np
import jax
import jax.numpy as jnp
from jax import lax
from jax.experimental import pallas as pl
from jax.experimental.pallas import tpu as pltpu

F32 = jnp.float32
BF16 = jnp.bfloat16
HIGHEST = lax.Precision.HIGHEST

D_MODEL = 2048
GROUP_WIDTH = D_MODEL // 4
N_HEADS = 4
HEAD_DIM = GROUP_WIDTH // 4
GLA_KEY_DIM = HEAD_DIM // 2
GLA_GATE_RANK = 16
GLA_GATE_TAU = 16.0
GDN_CONV = 4
POOL_WINDOWS = (2, 4, 8, 16)
CHUNK = 64
D_FF = 256 * int(math.ceil(8 * D_MODEL / 3 / 256))
FFN_CONV = 3
EPS = 1e-6

IN_SPLITS = (256, 256, 512, 512, 16, 1536, 512, 4, 4, 512, 512, 512, 512, 4)
IN_OFFS = tuple(int(sum(IN_SPLITS[:i])) for i in range(len(IN_SPLITS) + 1))

LANES = 128
P_FOX = 0
P_GLA = 1536
P_GDN = 3072
P_GDN_G = 4608
P_POOL = 5120
P_WIDTH = 5632
SM_GLA_LR = 0
SM_GDN_B = 16
SM_GDN_A = 20
SM_FOX_F = 28

VMEM_LIMIT = 56 * 1024 * 1024
FFN_VMEM_LIMIT = 60 * 1024 * 1024


def _cparams(n_axes):
    return pltpu.CompilerParams(dimension_semantics=("arbitrary",) * n_axes,
                                vmem_limit_bytes=VMEM_LIMIT)


def _dot(a, b, precision=None):
    return jnp.dot(a, b, preferred_element_type=F32, precision=precision)


def _dot_nt(a, b):
    return lax.dot_general(a, b, (((1,), (1,)), ((), ())), preferred_element_type=F32)


def _bdot(a, b):
    return lax.dot_general(a, b, (((2,), (1,)), ((0,), (0,))), preferred_element_type=F32)


def _bdot_nt(a, b):
    return lax.dot_general(a, b, (((2,), (2,)), ((0,), (0,))), preferred_element_type=F32)


def _dot_sel(sel_bf16, x):
    x1 = x.astype(BF16)
    r1 = x - x1.astype(F32)
    x2 = r1.astype(BF16)
    x3 = (r1 - x2.astype(F32)).astype(BF16)
    return _dot(sel_bf16, x1) + _dot(sel_bf16, x2) + _dot(sel_bf16, x3)


def _bdot_sel(sel_bf16, x):
    x1 = x.astype(BF16)
    r1 = x - x1.astype(F32)
    x2 = r1.astype(BF16)
    x3 = (r1 - x2.astype(F32)).astype(BF16)
    return _bdot(sel_bf16, x1) + _bdot(sel_bf16, x2) + _bdot(sel_bf16, x3)


def _log_sigmoid(x):
    return jnp.minimum(x, 0.0) - jnp.log1p(jnp.exp(-jnp.abs(x)))


def _softplus(x):
    return jnp.maximum(x, 0.0) + jnp.log1p(jnp.exp(-jnp.abs(x)))


def _silu(x):
    return x * jax.nn.sigmoid(x)


def _rms_mod(x, norm_w, shift, scale):
    y = x * lax.rsqrt(jnp.mean(x * x, axis=-1, keepdims=True) + EPS) * norm_w
    return y * (1.0 + scale) + shift


def _mod_kernel(c_ref, w_ref, b_ref, o_ref):
    cond = _silu(c_ref[...])
    o_ref[0] = _dot(cond, w_ref[0], precision=HIGHEST) + b_ref[0]


def _modulation(c_pad, w_mod, b_mod):
    L, D, N = w_mod.shape
    tn = 1024
    return pl.pallas_call(
        _mod_kernel,
        grid=(L, N // tn),
        in_specs=[pl.BlockSpec((8, D), lambda l, j: (0, 0)),
                  pl.BlockSpec((1, D, tn), lambda l, j: (l, 0, j)),
                  pl.BlockSpec((1, 1, tn), lambda l, j: (l, 0, j))],
        out_specs=pl.BlockSpec((1, 8, tn), lambda l, j: (l, 0, j)),
        out_shape=jax.ShapeDtypeStruct((L, 8, N), F32),
        compiler_params=_cparams(2),
        name="modulation",
    )(c_pad, w_mod, b_mod.reshape(L, 1, N))


def _wprep_kernel(w_ref, main_ref, small_ref):
    tk = w_ref.shape[2]
    nf = IN_SPLITS[13]
    assert (SM_GLA_LR, SM_GDN_B, SM_GDN_A, SM_FOX_F) == (0, 16, 20, 28) and nf == 4
    for l in range(w_ref.shape[1]):
        off = 0
        for i in (10, 11, 12, 0, 1, 2, 3, 5, 6, 9):
            width = IN_SPLITS[i]
            main_ref[l, :, off:off + width] = jnp.transpose(w_ref[IN_OFFS[i]:IN_OFFS[i + 1], l, :]).astype(BF16)
            off += width
        tail = w_ref[IN_OFFS[14] - 8:IN_OFFS[14], l, :]
        tail = jnp.where(lax.broadcasted_iota(jnp.int32, (8, tk), 0) >= 8 - nf, tail, 0.0)
        narrow = jnp.concatenate(
            [w_ref[IN_OFFS[4]:IN_OFFS[5], l, :],
             w_ref[IN_OFFS[7]:IN_OFFS[9], l, :],
             tail,
             jnp.zeros((LANES - 32, tk), F32)], axis=0)
        small_ref[l] = jnp.transpose(narrow).astype(BF16)


def _prepare_in_weights(w_in):
    L, D, N = w_in.shape
    tk = 128
    wt = jnp.transpose(w_in, (2, 0, 1))
    return pl.pallas_call(
        _wprep_kernel,
        grid=(D // tk,),
        in_specs=[pl.BlockSpec((N, L, tk), lambda r: (0, 0, r))],
        out_specs=[pl.BlockSpec((L, tk, P_WIDTH), lambda r: (0, r, 0)),
                   pl.BlockSpec((L, tk, LANES), lambda r: (0, r, 0))],
        out_shape=[jax.ShapeDtypeStruct((L, D, P_WIDTH), BF16),
                   jax.ShapeDtypeStruct((L, D, LANES), BF16)],
        compiler_params=_cparams(1),
        name="in_weight_prep",
    )(wt)


def _inproj_kernel(x_ref, mod_ref, nw_ref, w_ref, ws_ref, o_ref, os_ref, h_ref):
    @pl.when(pl.program_id(1) == 0)
    def _():
        h = _rms_mod(x_ref[...], nw_ref[...], mod_ref[0, 0:1, :], mod_ref[0, 1:2, :])
        h_ref[...] = h.astype(BF16)
        os_ref[...] = _dot(h_ref[...], ws_ref[0])

    o_ref[...] = _dot(h_ref[...], w_ref[0])


def _in_projection(x2, mod, norm_w, w_main, w_small, l, T):
    M, D = x2.shape
    N = w_main.shape[2]
    tm, tn = 2048, 512
    tpb = T // tm
    return pl.pallas_call(
        _inproj_kernel,
        grid=(M // tm, N // tn),
        in_specs=[pl.BlockSpec((tm, D), lambda i, j: (i, 0), pipeline_mode=pl.Buffered(1)),
                  pl.BlockSpec((1, 6, D), lambda i, j: (i // tpb, 0, 0)),
                  pl.BlockSpec((1, D), lambda i, j: (0, 0)),
                  pl.BlockSpec((1, D, tn), lambda i, j: (l, 0, j)),
                  pl.BlockSpec((1, D, LANES), lambda i, j: (l, 0, 0))],
        out_specs=[pl.BlockSpec((tm, tn), lambda i, j: (i, j)),
                   pl.BlockSpec((tm, LANES), lambda i, j: (i, 0))],
        out_shape=[jax.ShapeDtypeStruct((M, N), F32),
                   jax.ShapeDtypeStruct((M, LANES), F32)],
        scratch_shapes=[pltpu.VMEM((tm, D), BF16)],
        compiler_params=_cparams(2),
        name="in_projection",
    )(x2, mod, norm_w.reshape(1, D), w_main, w_small)


GLA_LEVELS = 7
GLA_GROUP = 4
GDN_GROUP = 4
GDN_PAIR = 2
GDN_LEVELS = 6


def _chunk_constants():
    i = np.arange(CHUNK)[:, None]
    j = np.arange(CHUNK)[None, :]
    tri = (i >= j).astype(np.float32)
    strict = (i > j).astype(np.float32)
    sels, masks = [], [(i == j).astype(np.float32)]
    for lv in range(1, GLA_LEVELS):
        m = 1 << (lv - 1)
        ref_row = (np.arange(CHUNK) // (2 * m)) * 2 * m + m
        sels.append((j == ref_row[:, None]).astype(np.float32))
        same = (i // (2 * m)) == (j // (2 * m))
        masks.append((same & ((i % (2 * m)) >= m) & ((j % (2 * m)) < m)).astype(np.float32))
    sel_all = np.concatenate(sels, axis=0)
    return tri, strict, sel_all, np.stack(masks)


def _gla_kernel(p_ref, sm_ref, wlr_ref, blr_ref, nw_ref, tri_ref, sel_ref, msk_ref, bd_ref, o_ref, s_ref):
    tb = p_ref.shape[0]
    dk, dv = GLA_KEY_DIM, HEAD_DIM
    kw = N_HEADS * dk
    G = GLA_GROUP
    R = G * CHUNK
    NP = N_HEADS // 2
    NB = NP * G
    pw = 2 * dk
    vw = 2 * dv

    @pl.when(pl.program_id(1) == 0)
    def _():
        s_ref[...] = jnp.zeros_like(s_ref)

    def to_problems(a, w):
        return jnp.concatenate([a[:, p * w:(p + 1) * w].reshape(G, CHUNK, w) for p in range(NP)], axis=0)

    def pair_block_diag(a, w):
        lane = lax.broadcasted_iota(jnp.int32, a.shape, 2)
        zero = jnp.zeros_like(a)
        return jnp.concatenate([jnp.where(lane < w, a, zero), jnp.where(lane >= w, a, zero)], axis=1)

    def group(i, carry):
        r0 = pl.multiple_of(i * R, R)
        rows = pl.ds(r0, R)
        lr = sm_ref[rows, SM_GLA_LR:SM_GLA_LR + GLA_GATE_RANK]
        logits = _dot(lr, wlr_ref[...], precision=HIGHEST) + blr_ref[...]
        logg = (_log_sigmoid(logits) * (1.0 / GLA_GATE_TAU)).reshape(G, CHUNK, kw)
        tri = jnp.broadcast_to(tri_ref[...][None], (G, CHUNK, CHUNK))
        b3 = _bdot_sel(tri, logg)
        sel = jnp.broadcast_to(sel_ref[...][None], (G,) + sel_ref.shape)
        refs = _bdot_sel(sel, b3)
        b = b3.reshape(R, kw)
        q = p_ref[rows, 0:kw] * dk ** -0.5
        k = p_ref[rows, kw:2 * kw]
        att = None
        for lv in range(GLA_LEVELS):
            if lv == 0:
                qt, kt = q, k
            else:
                r = refs[:, (lv - 1) * CHUNK:lv * CHUNK, :].reshape(R, kw)
                qt = q * jnp.exp(jnp.minimum(b - r, 0.0))
                kt = k * jnp.exp(jnp.minimum(r - b, 0.0))
            qp = to_problems(qt.astype(BF16), pw)
            kp = pair_block_diag(to_problems(kt.astype(BF16), pw), dk)
            a = _bdot_nt(qp, kp) * msk_ref[lv]
            att = a if att is None else att + a
        v = p_ref[rows, 2 * kw:2 * kw + N_HEADS * dv]
        vp = to_problems(v.astype(BF16), vw)
        o = _bdot(att.astype(BF16), pair_block_diag(vp, dv))

        blast = jnp.broadcast_to(b3[:, CHUNK - 1:CHUNK, :], (G, CHUNK, kw)).reshape(R, kw)
        qe = to_problems((q * jnp.exp(b)).astype(BF16), pw)
        kl = to_problems(k * jnp.exp(blast - b), pw)
        dcol = to_problems(jnp.exp(blast), pw)
        bd = bd_ref[...]
        for p in range(NP):
            st = s_ref[p]
            for g in range(G):
                n = p * G + g
                o_n = o[n] + _dot(qe[n], st.astype(BF16))
                upd = _dot(jnp.transpose(kl[n]).astype(BF16), vp[n]) * bd
                st = st * jnp.transpose(dcol[n])[:, 0:1] + upd
                for j in range(2):
                    h = 2 * p + j
                    oh = o_n[:, j * dv:(j + 1) * dv]
                    on = oh * lax.rsqrt(jnp.mean(oh * oh, axis=-1, keepdims=True) + EPS) * nw_ref[...]
                    gate = p_ref[pl.ds(r0 + g * CHUNK, CHUNK), 2 * kw + (N_HEADS + h) * dv:2 * kw + (N_HEADS + h + 1) * dv]
                    o_ref[pl.ds(r0 + g * CHUNK, CHUNK), h * dv:(h + 1) * dv] = (on * _silu(gate)).astype(BF16)
            s_ref[p] = st
        return carry

    lax.fori_loop(0, tb // R, group, 0)


def _gla_mixer(proj, small, w_lr, b_lr, norm_w, consts, B, T):
    tri, _, sel_all, masks = consts
    M = proj.shape[0]
    tb = 512
    nt = T // tb
    kw = N_HEADS * GLA_KEY_DIM
    masks2 = np.concatenate([masks, masks], axis=2)
    bd = np.kron(np.eye(2, dtype=np.float32), np.ones((GLA_KEY_DIM, HEAD_DIM), np.float32))
    const = lambda a: pl.BlockSpec(a.shape, lambda b, t: (0,) * a.ndim)
    return pl.pallas_call(
        _gla_kernel,
        grid=(B, nt),
        in_specs=[pl.BlockSpec((tb, 1536), lambda b, t: (b * nt + t, P_GLA // 1536)),
                  pl.BlockSpec((tb, LANES), lambda b, t: (b * nt + t, 0)),
                  pl.BlockSpec((GLA_GATE_RANK, kw), lambda b, t: (0, 0)),
                  pl.BlockSpec((1, kw), lambda b, t: (0, 0)),
                  pl.BlockSpec((1, HEAD_DIM), lambda b, t: (0, 0)),
                  const(tri), const(sel_all), const(masks2), const(bd)],
        out_specs=pl.BlockSpec((tb, GROUP_WIDTH), lambda b, t: (b * nt + t, 0)),
        out_shape=jax.ShapeDtypeStruct((M, GROUP_WIDTH), BF16),
        scratch_shapes=[pltpu.VMEM((N_HEADS // 2, 2 * GLA_KEY_DIM, 2 * HEAD_DIM), F32)],
        compiler_params=_cparams(2),
        name="gla_mixer",
    )(proj, small, w_lr, b_lr.reshape(1, kw), norm_w.reshape(1, HEAD_DIM),
      jnp.asarray(tri, BF16), jnp.asarray(sel_all, BF16), jnp.asarray(masks2, F32), jnp.asarray(bd, F32))


def _gdn_kernel(x_ref, g_ref, sm_ref, cw_ref, alog_ref, dtb_ref, nw_ref, tri_ref, incl_ref, lvl_ref,
                o_ref, s_ref, halo_ref, cq_ref, ku_ref, au_ref, gl_ref):
    tb = x_ref.shape[0]
    d = HEAD_DIM
    hw = N_HEADS * d
    G = GDN_GROUP
    R = G * CHUNK
    P = GDN_PAIR
    BR = P * CHUNK
    NP = N_HEADS // P

    @pl.when(pl.program_id(1) == 0)
    def _():
        s_ref[...] = jnp.zeros_like(s_ref)
        halo_ref[...] = jnp.zeros_like(halo_ref)

    def stack(a):
        w = a.shape[1] // N_HEADS
        head = lambda h: a[:, h * w:(h + 1) * w].reshape(G, CHUNK, w)
        return jnp.concatenate([jnp.concatenate([head(p * P + j) for j in range(P)], axis=1)
                                for p in range(NP)], axis=0)

    def gate_col(a, lane, last=False):
        def head(h):
            col = a[:, lane + h:lane + h + 1].reshape(G, CHUNK, 1)
            return jnp.broadcast_to(col[:, CHUNK - 1:CHUNK, :], (G, CHUNK, 1)) if last else col
        return jnp.concatenate([jnp.concatenate([head(p * P + j) for j in range(P)], axis=1)
                                for p in range(NP)], axis=0)

    incl = incl_ref[...]
    head_of_row = lax.broadcasted_iota(jnp.int32, (BR, d), 0) // CHUNK

    def block_diag(a):
        return jnp.concatenate([jnp.where(head_of_row == j, a, 0.0) for j in range(P)], axis=2)

    def prepare(i, carry):
        r0 = pl.multiple_of(i * R, R)
        rows = pl.ds(r0, R)
        prev = x_ref[pl.ds(pl.multiple_of(jnp.maximum(r0 - 8, 0), 8), 8), :]
        prev = jnp.where(i == 0, halo_ref[...], prev)
        win = jnp.concatenate([prev, x_ref[rows, :]], axis=0)
        conv = win * cw_ref[GDN_CONV - 1:GDN_CONV, :]
        for s in range(1, GDN_CONV):
            conv = conv + pltpu.roll(win, s, axis=0) * cw_ref[GDN_CONV - 1 - s:GDN_CONV - s, :]
        qkv = _silu(conv[8:, :])

        sm = sm_ref[rows, :]
        beta = jax.nn.sigmoid(sm)
        g = -jnp.exp(alog_ref[...]) * _softplus(sm + dtb_ref[...])
        gcum = _dot_sel(tri_ref[...], g)
        for j in range(G):
            gl_ref[i * G + j] = jnp.broadcast_to(jnp.exp(gcum[(j + 1) * CHUNK - 1:(j + 1) * CHUNK, :]), (8, LANES))

        qs, ks, vs = stack(qkv[:, 0:hw]), stack(qkv[:, hw:2 * hw]), stack(qkv[:, 2 * hw:3 * hw])
        qn = qs * lax.rsqrt(jnp.sum(qs * qs, axis=-1, keepdims=True) + EPS) * d ** -0.5
        kn = ks * lax.rsqrt(jnp.sum(ks * ks, axis=-1, keepdims=True) + EPS)
        bcol = gate_col(beta, SM_GDN_B)
        gcol = gate_col(gcum, SM_GDN_A)
        gmat = jnp.broadcast_to(gcol, (NP * G, BR, BR))
        grow = jnp.stack([jnp.transpose(gmat[b]) for b in range(NP * G)])
        decay = jnp.exp(jnp.minimum(gcol - grow, 0.0)) * incl
        kb = kn * bcol
        kn16 = kn.astype(BF16)
        lmat = _bdot_nt(kb.astype(BF16), kn16) * decay
        tinv = lvl_ref[GDN_LEVELS] - lmat * lvl_ref[0]
        for lv in range(1, GDN_LEVELS):
            t16 = tinv.astype(BF16)
            tl = _bdot(t16, (lmat * lvl_ref[lv]).astype(BF16))
            tinv = tinv - _bdot(tl.astype(BF16), t16)
        rhs = jnp.concatenate([vs * bcol, kb * jnp.exp(gcol)], axis=2)
        sol = _bdot(tinv.astype(BF16), rhs.astype(BF16))
        sol16 = sol.astype(BF16)
        attn = _bdot_nt(qn.astype(BF16), kn16) * decay
        aw = _bdot(attn.astype(BF16), sol16)
        qp = (qn * jnp.exp(gcol) - aw[:, :, d:]).astype(BF16)
        kdec = kn * jnp.exp(gate_col(gcum, SM_GDN_A, last=True) - gcol)
        kd = block_diag(kdec)
        for p in range(NP):
            for g_ in range(G):
                b, c = p * G + g_, i * G + g_
                kt = jnp.transpose(kd[b]).astype(BF16)
                ku = _dot(kt, sol16[b])
                for j in range(P):
                    h = p * P + j
                    ku_ref[c, h * d:(h + 1) * d, :] = ku[j * d:(j + 1) * d, :d]
                    au_ref[c, h * CHUNK:(h + 1) * CHUNK, :] = aw[b, j * CHUNK:(j + 1) * CHUNK, :d]
                    cq_ref[c * N_HEADS + h, 0:d, :] = ku[j * d:(j + 1) * d, d:].astype(BF16)
                    cq_ref[c * N_HEADS + h, d:d + CHUNK, :] = qp[b, j * CHUNK:(j + 1) * CHUNK, :]
        return carry

    lax.fori_loop(0, tb // R, prepare, 0)
    halo_ref[...] = x_ref[tb - 8:tb, :]

    def recur(c, carry):
        rows = pl.ds(pl.multiple_of(c * CHUNK, CHUNK), CHUNK)
        gl = gl_ref[c]
        for h in range(N_HEADS):
            hs = slice(h * d, (h + 1) * d)
            st = s_ref[h]
            r = _dot(cq_ref[c * N_HEADS + h], st.astype(BF16))
            s_ref[h] = st * gl[0:1, SM_GDN_A + h:SM_GDN_A + h + 1] - r[:d] + ku_ref[c, hs, :]
            oh = r[d:] + au_ref[c, h * CHUNK:(h + 1) * CHUNK, :]
            on = oh * lax.rsqrt(jnp.mean(oh * oh, axis=-1, keepdims=True) + EPS) * nw_ref[...]
            o_ref[rows, hs] = (on * _silu(g_ref[rows, hs])).astype(BF16)
        return carry

    lax.fori_loop(0, tb // CHUNK, recur, 0)


def _gdn_mixer(proj, small, conv_w, a_log, dt_bias, norm_w, consts, B, T):
    tri, _, _, masks = consts
    M = proj.shape[0]
    tb = 512
    nt = T // tb
    nc = tb // CHUNK
    hb = N_HEADS * CHUNK
    pad = lambda v: jnp.zeros((1, LANES), F32).at[0, SM_GDN_A:SM_GDN_A + N_HEADS].set(v)
    tri_g = np.kron(np.eye(GDN_GROUP, dtype=np.float32), tri)
    incl_bd = np.kron(np.eye(GDN_PAIR, dtype=np.float32), tri)
    eye_p = np.eye(GDN_PAIR, dtype=np.float32)
    lvl = np.stack([np.kron(eye_p, m) for m in masks[1:]] + [np.eye(GDN_PAIR * CHUNK, dtype=np.float32)])
    assert lvl.shape[0] == GDN_LEVELS + 1
    const = lambda a: pl.BlockSpec(a.shape, lambda b, t: (0,) * a.ndim)
    return pl.pallas_call(
        _gdn_kernel,
        grid=(B, nt),
        in_specs=[pl.BlockSpec((tb, 1536), lambda b, t: (b * nt + t, P_GDN // 1536)),
                  pl.BlockSpec((tb, GROUP_WIDTH), lambda b, t: (b * nt + t, P_GDN_G // GROUP_WIDTH)),
                  pl.BlockSpec((tb, LANES), lambda b, t: (b * nt + t, 0)),
                  pl.BlockSpec((GDN_CONV, 1536), lambda b, t: (0, 0)),
                  pl.BlockSpec((1, LANES), lambda b, t: (0, 0)),
                  pl.BlockSpec((1, LANES), lambda b, t: (0, 0)),
                  pl.BlockSpec((1, HEAD_DIM), lambda b, t: (0, 0)),
                  const(tri_g), const(incl_bd), const(lvl)],
        out_specs=pl.BlockSpec((tb, GROUP_WIDTH), lambda b, t: (b * nt + t, 0)),
        out_shape=jax.ShapeDtypeStruct((M, GROUP_WIDTH), BF16),
        scratch_shapes=[pltpu.VMEM((N_HEADS, HEAD_DIM, HEAD_DIM), F32),
                        pltpu.VMEM((8, 1536), F32),
                        pltpu.VMEM((nc * N_HEADS, HEAD_DIM + CHUNK, HEAD_DIM), BF16),
                        pltpu.VMEM((nc, N_HEADS * HEAD_DIM, HEAD_DIM), F32),
                        pltpu.VMEM((nc, hb, HEAD_DIM), F32),
                        pltpu.VMEM((nc, 8, LANES), F32)],
        compiler_params=_cparams(2),
        name="gdn_mixer",
    )(proj, proj, small, conv_w, pad(a_log), pad(dt_bias), norm_w.reshape(1, HEAD_DIM),
      jnp.asarray(tri_g, BF16), jnp.asarray(incl_bd, F32), jnp.asarray(lvl, F32))


POOL_HALO = 16


def _pool_kernel(u_ref, w_ref, sc_ref, o_ref, halo_ref):
    tb = u_ref.shape[0]
    gd = GROUP_WIDTH // len(POOL_WINDOWS)
    t = pl.program_id(1)

    @pl.when(t == 0)
    def _():
        halo_ref[...] = jnp.zeros_like(halo_ref)

    pos = (t * tb + 1 + lax.broadcasted_iota(jnp.int32, (tb, 1), 0)).astype(F32)
    for gi, win in enumerate(POOL_WINDOWS):
        cols = slice(gi * gd, (gi + 1) * gd)
        u = u_ref[:, cols]
        ssum = jnp.concatenate([halo_ref[:, cols], u], axis=0)
        shift = 1
        while shift < win:
            ssum = ssum + pltpu.roll(ssum, shift, axis=0)
            shift *= 2
        mean = ssum[POOL_HALO:, :] / jnp.minimum(pos, float(win))
        y = _dot((mean - u).astype(BF16), w_ref[gi])
        o_ref[:, cols] = (y * sc_ref[:, cols]).astype(BF16)
    halo_ref[...] = u_ref[tb - POOL_HALO:tb, :]


def _pool_mixer(proj, w_bf16, scale, B, T):
    M = proj.shape[0]
    tb = 512
    nt = T // tb
    gd = GROUP_WIDTH // len(POOL_WINDOWS)
    return pl.pallas_call(
        _pool_kernel,
        grid=(B, nt),
        in_specs=[pl.BlockSpec((tb, GROUP_WIDTH), lambda b, t: (b * nt + t, P_POOL // GROUP_WIDTH)),
                  pl.BlockSpec((len(POOL_WINDOWS), gd, gd), lambda b, t: (0, 0, 0)),
                  pl.BlockSpec((1, GROUP_WIDTH), lambda b, t: (0, 0))],
        out_specs=pl.BlockSpec((tb, GROUP_WIDTH), lambda b, t: (b * nt + t, 0)),
        out_shape=jax.ShapeDtypeStruct((M, GROUP_WIDTH), BF16),
        scratch_shapes=[pltpu.VMEM((POOL_HALO, GROUP_WIDTH), F32)],
        compiler_params=_cparams(2),
        name="pool_mixer",
    )(proj, w_bf16, scale.reshape(1, GROUP_WIDTH))


FOX_CUM_BLOCK = 512


def _fox_gate_kernel(sm_ref, bias_ref, tri_ref, o_ref, carry_ref):
    @pl.when(pl.program_id(1) == 0)
    def _():
        carry_ref[...] = jnp.zeros_like(carry_ref)

    lf = _log_sigmoid(sm_ref[...] + bias_ref[...])
    cum = _dot_sel(tri_ref[...], lf) + carry_ref[...]
    carry_ref[...] = cum[FOX_CUM_BLOCK - 1:FOX_CUM_BLOCK, :]
    o_ref[...] = cum


def _fox_gate_cumsum(small, f_bias, B, T):
    tb = FOX_CUM_BLOCK
    nt = T // tb
    tri = jnp.asarray(np.tril(np.ones((tb, tb), np.float32)), BF16)
    bias = jnp.zeros((1, LANES), F32).at[0, SM_FOX_F:SM_FOX_F + N_HEADS].set(f_bias)
    return pl.pallas_call(
        _fox_gate_kernel,
        grid=(B, nt),
        in_specs=[pl.BlockSpec((tb, LANES), lambda b, t: (b * nt + t, 0)),
                  pl.BlockSpec((1, LANES), lambda b, t: (0, 0)),
                  pl.BlockSpec((tb, tb), lambda b, t: (0, 0))],
        out_specs=pl.BlockSpec((tb, LANES), lambda b, t: (b * nt + t, 0)),
        out_shape=jax.ShapeDtypeStruct((B * T, LANES), F32),
        scratch_shapes=[pltpu.VMEM((1, LANES), F32)],
        compiler_params=_cparams(2),
        name="fox_gate_cumsum",
    )(small, bias, tri)


FOX_BLOCK = 512
FOX_KEY_BLOCK = 512


LOG2E = math.log2(math.e)


def _fox_kernel(q_ref, k_ref, v_ref, f_ref, o_ref, kb_ref, vt_ref):
    qi = pl.program_id(1)
    blk = FOX_BLOCK
    d = HEAD_DIM
    T = k_ref.shape[0]

    @pl.when(qi == 0)
    def _():
        kb_ref[...] = k_ref[...].astype(BF16)
        for h in range(N_HEADS):
            for c in range(T // blk):
                vt_ref[h, :, c * blk:(c + 1) * blk] = jnp.transpose(
                    v_ref[c * blk:(c + 1) * blk, h * d:(h + 1) * d]).astype(BF16)

    q0 = pl.multiple_of(qi * blk, blk)
    qt = [jnp.transpose(q_ref[:, h * d:(h + 1) * d] * (d ** -0.5 * LOG2E)).astype(BF16)
          for h in range(N_HEADS)]
    fcol = lambda h, start, n: f_ref[pl.ds(start, n), SM_FOX_F + h:SM_FOX_F + h + 1]
    f0 = [fcol(h, q0, 8)[0:1, :] for h in range(N_HEADS)]

    kblk = FOX_KEY_BLOCK
    ratio = blk // kblk

    def block(ki, carry, masked):
        k0 = pl.multiple_of(ki * kblk, kblk)
        scores = [_dot(kb_ref[pl.ds(k0, kblk), h * d:(h + 1) * d], qt[h]) for h in range(N_HEADS)]
        stats, probs = [], []
        for h in range(N_HEADS):
            m, l, _ = carry[h]
            s = scores[h] - (fcol(h, k0, kblk) - f0[h]) * LOG2E
            if masked:
                key = k0 + lax.broadcasted_iota(jnp.int32, (kblk, blk), 0)
                qry = q0 + lax.broadcasted_iota(jnp.int32, (kblk, blk), 1)
                s = jnp.where(key <= qry, s, -jnp.inf)
            m_new = jnp.maximum(m, jnp.max(s, axis=0, keepdims=True))
            alpha = jnp.exp2(m - m_new)
            p = jnp.exp2(s - m_new)
            stats.append((m_new, alpha, alpha * l + jnp.sum(p, axis=0, keepdims=True)))
            probs.append(p.astype(BF16))
        out = []
        for h in range(N_HEADS):
            m_new, alpha, l = stats[h]
            acc = alpha * carry[h][2] + _dot(vt_ref[h, :, pl.ds(k0, kblk)], probs[h])
            out.append((m_new, l, acc))
        return tuple(out)

    init = tuple((jnp.full((1, blk), -jnp.inf, F32), jnp.zeros((1, blk), F32), jnp.zeros((d, blk), F32))
                 for _ in range(N_HEADS))
    carry = lax.fori_loop(0, qi * ratio, lambda ki, c: block(ki, c, False), init)
    for j in range(ratio):
        carry = block(qi * ratio + j, carry, True)
    for h in range(N_HEADS):
        _, l, acc = carry[h]
        o_ref[:, h * d:(h + 1) * d] = jnp.transpose(acc / l).astype(BF16)


def _fox_attention(proj, fcum, B, T):
    M = proj.shape[0]
    blk = FOX_BLOCK
    nq = T // blk
    return pl.pallas_call(
        _fox_kernel,
        grid=(B, nq),
        in_specs=[pl.BlockSpec((blk, GROUP_WIDTH), lambda b, i: (b * nq + i, P_FOX // GROUP_WIDTH)),
                  pl.BlockSpec((T, GROUP_WIDTH), lambda b, i: (b, P_FOX // GROUP_WIDTH + 1)),
                  pl.BlockSpec((T, GROUP_WIDTH), lambda b, i: (b, P_FOX // GROUP_WIDTH + 2)),
                  pl.BlockSpec((T, LANES), lambda b, i: (b, 0))],
        out_specs=pl.BlockSpec((blk, GROUP_WIDTH), lambda b, i: (b * nq + i, 0)),
        out_shape=jax.ShapeDtypeStruct((M, GROUP_WIDTH), BF16),
        scratch_shapes=[pltpu.VMEM((T, GROUP_WIDTH), BF16),
                        pltpu.VMEM((N_HEADS, HEAD_DIM, T), BF16)],
        compiler_params=_cparams(2),
        name="fox_attention",
    )(proj, proj, proj, fcum)


def _outproj_kernel(ya_ref, yb_ref, yc_ref, yd_ref, w_ref, x_ref, mod_ref, o_ref):
    gw = GROUP_WIDTH
    acc = _dot(ya_ref[...], w_ref[0:gw, :])
    acc = acc + _dot(yb_ref[...], w_ref[gw:2 * gw, :])
    acc = acc + _dot(yc_ref[...], w_ref[2 * gw:3 * gw, :])
    acc = acc + _dot(yd_ref[...], w_ref[3 * gw:4 * gw, :])
    o_ref[...] = x_ref[...] + mod_ref[0, 2:3, :] * acc


def _out_projection(ys, w_bf16, x2, mod, T):
    M, D = x2.shape
    tm, tn = 1024, 1024
    tpb = T // tm
    yspec = pl.BlockSpec((tm, GROUP_WIDTH), lambda i, j: (i, 0))
    return pl.pallas_call(
        _outproj_kernel,
        grid=(M // tm, D // tn),
        in_specs=[yspec, yspec, yspec, yspec,
                  pl.BlockSpec((D, tn), lambda i, j: (0, j)),
                  pl.BlockSpec((tm, tn), lambda i, j: (i, j)),
                  pl.BlockSpec((1, 6, tn), lambda i, j: (i // tpb, 0, j))],
        out_specs=pl.BlockSpec((tm, tn), lambda i, j: (i, j)),
        out_shape=jax.ShapeDtypeStruct((M, D), F32),
        compiler_params=_cparams(2),
        name="out_projection",
    )(*ys, w_bf16, x2, mod)


def _ffn_kernel(x_ref, mod_ref, nw_ref, wg_ref, wu_ref, cw_ref, cb_ref, wd_ref, fw_ref, o_ref,
                h_ref, halo_ref, *, tiles_per_seq, final_norm):
    i = pl.program_id(0)
    j = pl.program_id(1)
    tm, tf = x_ref.shape[0], wg_ref.shape[2]

    @pl.when(j == 0)
    def _():
        x = x_ref[...]
        h = _rms_mod(x, nw_ref[...], mod_ref[0, 3:4, :], mod_ref[0, 4:5, :])
        h_ref[...] = h.astype(BF16)
        o_ref[...] = x

    h = h_ref[...]
    gate = _dot(h, wg_ref[0])
    up = _dot(h, wu_ref[0])
    prev = jnp.where(i % tiles_per_seq == 0, 0.0, halo_ref[j])
    row = lax.broadcasted_iota(jnp.int32, (tm, tf), 0)
    g1 = jnp.where(row == 0, prev[7:8, :], pltpu.roll(gate, 1, axis=0))
    g2 = jnp.where(row == 0, prev[6:7, :], jnp.where(row == 1, prev[7:8, :], pltpu.roll(gate, 2, axis=0)))
    halo_ref[j] = gate[tm - 8:tm, :]
    conv = cw_ref[0, 0:1, :] * g2 + cw_ref[0, 1:2, :] * g1 + cw_ref[0, 2:3, :] * gate + cb_ref[0]
    act = (_silu(conv) * up).astype(BF16)
    o_ref[...] += mod_ref[0, 5:6, :] * _dot(act, wd_ref[0])

    if final_norm:
        @pl.when(j == pl.num_programs(1) - 1)
        def _():
            y = o_ref[...]
            o_ref[...] = y * lax.rsqrt(jnp.mean(y * y, axis=-1, keepdims=True) + EPS) * fw_ref[...]


def _conv_ffn(x2, mod, norm_w, wg, wu, conv_w, conv_b, wd, final_w, l, T):
    M, D = x2.shape
    L, _, F = wg.shape
    tm, tf = 1024, 512
    tpb = T // tm
    return pl.pallas_call(
        functools.partial(_ffn_kernel, tiles_per_seq=tpb, final_norm=(l == L - 1)),
        grid=(M // tm, F // tf),
        in_specs=[pl.BlockSpec((tm, D), lambda i, j: (i, 0), pipeline_mode=pl.Buffered(1)),
                  pl.BlockSpec((1, 6, D), lambda i, j: (i // tpb, 0, 0)),
                  pl.BlockSpec((1, D), lambda i, j: (0, 0)),
                  pl.BlockSpec((1, D, tf), lambda i, j: (l, 0, j)),
                  pl.BlockSpec((1, D, tf), lambda i, j: (l, 0, j)),
                  pl.BlockSpec((1, FFN_CONV, tf), lambda i, j: (l, 0, j)),
                  pl.BlockSpec((1, 1, tf), lambda i, j: (l, 0, j)),
                  pl.BlockSpec((1, tf, D), lambda i, j: (l, j, 0)),
                  pl.BlockSpec((1, D), lambda i, j: (0, 0))],
        out_specs=pl.BlockSpec((tm, D), lambda i, j: (i, 0)),
        out_shape=jax.ShapeDtypeStruct((M, D), F32),
        scratch_shapes=[pltpu.VMEM((tm, D), BF16),
                        pltpu.VMEM((F // tf, 8, tf), F32)],
        compiler_params=pltpu.CompilerParams(dimension_semantics=("arbitrary", "arbitrary"),
                                             vmem_limit_bytes=FFN_VMEM_LIMIT),
        name="conv_ffn",
    )(x2, mod, norm_w.reshape(1, D), wg, wu, conv_w, conv_b.reshape(L, 1, F), wd, final_w.reshape(1, D))


def kernel(x, c, w_mod, b_mod, norm_mix, norm_ffn, w_in, gla_w_lr, gla_b_lr, gla_norm, gdn_conv, gdn_a_log, gdn_dt_bias, gdn_norm, pool_w, pool_scale, fox_f_bias, w_out, ffn_w_gate, ffn_w_up, ffn_conv_w, ffn_conv_b, ffn_w_down, norm_final):
    B, T, D = x.shape
    L = w_mod.shape[0]
    assert D == D_MODEL and T % 2048 == 0 and B <= 8
    consts = _chunk_constants()

    c_pad = jnp.zeros((8, D), F32).at[:B].set(c)
    mod_all = _modulation(c_pad, w_mod, b_mod)
    w_main, w_small = _prepare_in_weights(w_in)
    w_gate, w_up, w_down = ffn_w_gate.astype(BF16), ffn_w_up.astype(BF16), ffn_w_down.astype(BF16)
    x2 = x.reshape(B * T, D)
    for l in range(L):
        mod = mod_all[l, :B].reshape(B, 6, D)
        proj, small = _in_projection(x2, mod, norm_mix[l], w_main, w_small, l, T)
        y_a = _gla_mixer(proj, small, gla_w_lr[l], gla_b_lr[l], gla_norm[l], consts, B, T)
        y_b = _gdn_mixer(proj, small, gdn_conv[l], gdn_a_log[l], gdn_dt_bias[l], gdn_norm[l], consts, B, T)
        y_c = _pool_mixer(proj, pool_w[l].astype(BF16), pool_scale[l], B, T)
        y_d = _fox_attention(proj, _fox_gate_cumsum(small, fox_f_bias[l], B, T), B, T)
        x2 = _out_projection((y_a, y_b, y_c, y_d), w_out[l].astype(BF16), x2, mod, T)
        x2 = _conv_ffn(x2, mod, norm_ffn[l], w_gate, w_up, ffn_conv_w, ffn_conv_b, w_down, norm_final, l, T)
    return x2.reshape(B, T, D)
```

```python
import functools
import math

import numpy as np
import jax
import jax.numpy as jnp
from jax import lax
from jax.experimental import pallas as pl
from jax.experimental.pallas import tpu as pltpu

F32 = jnp.float32
BF16 = jnp.bfloat16
HIGHEST = lax.Precision.HIGHEST

D_MODEL = 2048
GROUP_WIDTH = D_MODEL // 4
N_HEADS = 4
HEAD_DIM = GROUP_WIDTH // 4
GLA_KEY_DIM = HEAD_DIM // 2
GLA_GATE_RANK = 16
GLA_GATE_TAU = 16.0
GDN_CONV = 4
POOL_WINDOWS = (2, 4, 8, 16)
CHUNK = 64
D_FF = 256 * int(math.ceil(8 * D_MODEL / 3 / 256))
FFN_CONV = 3
EPS = 1e-6

IN_SPLITS = (256, 256, 512, 512, 16, 1536, 512, 4, 4, 512, 512, 512, 512, 4)
IN_OFFS = tuple(int(sum(IN_SPLITS[:i])) for i in range(len(IN_SPLITS) + 1))

LANES = 128
P_FOX = 0
P_GLA = 1536
P_GDN = 3072
P_GDN_G = 4608
P_POOL = 5120
P_WIDTH = 5632
SM_GLA_LR = 0
SM_GDN_B = 16
SM_GDN_A = 20
SM_FOX_F = 28

VMEM_LIMIT = 56 * 1024 * 1024
FFN_VMEM_LIMIT = 60 * 1024 * 1024


def _cparams(n_axes):
    return pltpu.CompilerParams(dimension_semantics=("arbitrary",) * n_axes,
                                vmem_limit_bytes=VMEM_LIMIT)


def _dot(a, b, precision=None):
    return jnp.dot(a, b, preferred_element_type=F32, precision=precision)


def _dot_nt(a, b):
    return lax.dot_general(a, b, (((1,), (1,)), ((), ())), preferred_element_type=F32)


def _bdot(a, b):
    return lax.dot_general(a, b, (((2,), (1,)), ((0,), (0,))), preferred_element_type=F32)


def _bdot_nt(a, b):
    return lax.dot_general(a, b, (((2,), (2,)), ((0,), (0,))), preferred_element_type=F32)


def _dot_sel(sel_bf16, x):
    x1 = x.astype(BF16)
    r1 = x - x1.astype(F32)
    x2 = r1.astype(BF16)
    x3 = (r1 - x2.astype(F32)).astype(BF16)
    return _dot(sel_bf16, x1) + _dot(sel_bf16, x2) + _dot(sel_bf16, x3)


def _bdot_sel(sel_bf16, x):
    x1 = x.astype(BF16)
    r1 = x - x1.astype(F32)
    x2 = r1.astype(BF16)
    x3 = (r1 - x2.astype(F32)).astype(BF16)
    return _bdot(sel_bf16, x1) + _bdot(sel_bf16, x2) + _bdot(sel_bf16, x3)


def _log_sigmoid(x):
    return jnp.minimum(x, 0.0) - jnp.log1p(jnp.exp(-jnp.abs(x)))


def _softplus(x):
    return jnp.maximum(x, 0.0) + jnp.log1p(jnp.exp(-jnp.abs(x)))


def _silu(x):
    return x * jax.nn.sigmoid(x)


def _rms_mod(x, norm_w, shift, scale):
    gain = norm_w * (1.0 + scale)
    return x * lax.rsqrt(jnp.mean(x * x, axis=-1, keepdims=True) + EPS) * gain + shift


def _cast_kernel(w_ref, o_ref):
    o_ref[...] = w_ref[...].astype(BF16)


def _to_bf16(w, rows):
    L, R, C = w.shape
    spec = pl.BlockSpec((1, rows, C), lambda l, r: (l, r, 0))
    return pl.pallas_call(
        _cast_kernel,
        grid=(L, R // rows),
        in_specs=[spec],
        out_specs=spec,
        out_shape=jax.ShapeDtypeStruct(w.shape, BF16),
        compiler_params=_cparams(2),
        name="weight_to_bf16",
    )(w)


def _mod_kernel(c_ref, w_ref, b_ref, o_ref):
    cond = _silu(c_ref[...])
    c_hi = cond.astype(BF16)
    c_lo = (cond - c_hi.astype(F32)).astype(BF16)
    w = w_ref[0]
    w_hi = w.astype(BF16)
    w_lo = (w - w_hi.astype(F32)).astype(BF16)
    both = _dot(jnp.concatenate([c_hi, c_lo], axis=0), w_hi)
    o_ref[0] = both[:8] + both[8:] + _dot(c_hi, w_lo) + b_ref[0]


def _modulation(c_pad, w_mod, b_mod):
    L, D, N = w_mod.shape
    tn = 1024
    return pl.pallas_call(
        _mod_kernel,
        grid=(L, N // tn),
        in_specs=[pl.BlockSpec((8, D), lambda l, j: (0, 0)),
                  pl.BlockSpec((1, D, tn), lambda l, j: (l, 0, j)),
                  pl.BlockSpec((1, 1, tn), lambda l, j: (l, 0, j))],
        out_specs=pl.BlockSpec((1, 8, tn), lambda l, j: (l, 0, j)),
        out_shape=jax.ShapeDtypeStruct((L, 8, N), F32),
        compiler_params=_cparams(2),
        name="modulation",
    )(c_pad, w_mod, b_mod.reshape(L, 1, N))


def _wprep_kernel(w_ref, main_ref, small_ref):
    tk = w_ref.shape[2]
    nf = IN_SPLITS[13]
    assert (SM_GLA_LR, SM_GDN_B, SM_GDN_A, SM_FOX_F) == (0, 16, 20, 28) and nf == 4
    for l in range(w_ref.shape[1]):
        off = 0
        for i in (10, 11, 12, 0, 1, 2, 3, 5, 6, 9):
            width = IN_SPLITS[i]
            main_ref[l, :, off:off + width] = jnp.transpose(w_ref[IN_OFFS[i]:IN_OFFS[i + 1], l, :]).astype(BF16)
            off += width
        tail = w_ref[IN_OFFS[14] - 8:IN_OFFS[14], l, :]
        tail = jnp.where(lax.broadcasted_iota(jnp.int32, (8, tk), 0) >= 8 - nf, tail, 0.0)
        narrow = jnp.concatenate(
            [w_ref[IN_OFFS[4]:IN_OFFS[5], l, :],
             w_ref[IN_OFFS[7]:IN_OFFS[9], l, :],
             tail,
             jnp.zeros((LANES - 32, tk), F32)], axis=0)
        small_ref[l] = jnp.transpose(narrow).astype(BF16)


def _prepare_in_weights(w_in):
    L, D, N = w_in.shape
    tk = 128
    wt = jnp.transpose(w_in, (2, 0, 1))
    return pl.pallas_call(
        _wprep_kernel,
        grid=(D // tk,),
        in_specs=[pl.BlockSpec((N, L, tk), lambda r: (0, 0, r))],
        out_specs=[pl.BlockSpec((L, tk, P_WIDTH), lambda r: (0, r, 0)),
                   pl.BlockSpec((L, tk, LANES), lambda r: (0, r, 0))],
        out_shape=[jax.ShapeDtypeStruct((L, D, P_WIDTH), BF16),
                   jax.ShapeDtypeStruct((L, D, LANES), BF16)],
        compiler_params=_cparams(1),
        name="in_weight_prep",
    )(wt)


def _inproj_kernel(x_ref, mod_ref, nw_ref, w_ref, ws_ref, o_ref, os_ref, h_ref):
    @pl.when(pl.program_id(1) == 0)
    def _():
        h = _rms_mod(x_ref[...], nw_ref[...], mod_ref[0, 0:1, :], mod_ref[0, 1:2, :])
        h_ref[...] = h.astype(BF16)
        os_ref[...] = _dot(h_ref[...], ws_ref[0])

    o_ref[...] = _dot(h_ref[...], w_ref[0])


def _in_projection(x2, mod, norm_w, w_main, w_small, l, T):
    M, D = x2.shape
    N = w_main.shape[2]
    tm, tn = 2048, 512
    tpb = T // tm
    return pl.pallas_call(
        _inproj_kernel,
        grid=(M // tm, N // tn),
        in_specs=[pl.BlockSpec((tm, D), lambda i, j: (i, 0), pipeline_mode=pl.Buffered(1)),
                  pl.BlockSpec((1, 6, D), lambda i, j: (i // tpb, 0, 0)),
                  pl.BlockSpec((1, D), lambda i, j: (0, 0)),
                  pl.BlockSpec((1, D, tn), lambda i, j: (l, 0, j)),
                  pl.BlockSpec((1, D, LANES), lambda i, j: (l, 0, 0))],
        out_specs=[pl.BlockSpec((tm, tn), lambda i, j: (i, j)),
                   pl.BlockSpec((tm, LANES), lambda i, j: (i, 0))],
        out_shape=[jax.ShapeDtypeStruct((M, N), F32),
                   jax.ShapeDtypeStruct((M, LANES), F32)],
        scratch_shapes=[pltpu.VMEM((tm, D), BF16)],
        compiler_params=_cparams(2),
        name="in_projection",
    )(x2, mod, norm_w.reshape(1, D), w_main, w_small)


GLA_LEVELS = 7
GLA_GROUP = 8
GDN_GROUP = 8
GDN_PAIR = 2
GDN_LEVELS = 6


def _chunk_constants():
    i = np.arange(CHUNK)[:, None]
    j = np.arange(CHUNK)[None, :]
    tri = (i >= j).astype(np.float32)
    strict = (i > j).astype(np.float32)
    sels, masks = [], [(i == j).astype(np.float32)]
    for lv in range(1, GLA_LEVELS):
        m = 1 << (lv - 1)
        ref_row = (np.arange(CHUNK) // (2 * m)) * 2 * m + m
        sels.append((j == ref_row[:, None]).astype(np.float32))
        same = (i // (2 * m)) == (j // (2 * m))
        masks.append((same & ((i % (2 * m)) >= m) & ((j % (2 * m)) < m)).astype(np.float32))
    sel_all = np.concatenate(sels, axis=0)
    return tri, strict, sel_all, np.stack(masks)


def _gla_kernel(p_ref, sm_ref, wlr_ref, blr_ref, nw_ref, tri_ref, sel_ref, msk_ref, bd_ref, o_ref, s_ref):
    tb = p_ref.shape[0]
    dk, dv = GLA_KEY_DIM, HEAD_DIM
    kw = N_HEADS * dk
    G = GLA_GROUP
    R = G * CHUNK
    NP = N_HEADS // 2
    NB = NP * G
    pw = 2 * dk
    vw = 2 * dv

    @pl.when(pl.program_id(1) == 0)
    def _():
        s_ref[...] = jnp.zeros_like(s_ref)

    def to_problems(a, w):
        return jnp.concatenate([a[:, p * w:(p + 1) * w].reshape(G, CHUNK, w) for p in range(NP)], axis=0)

    def pair_block_diag(a, w):
        lane = lax.broadcasted_iota(jnp.int32, a.shape, 2)
        zero = jnp.zeros_like(a)
        return jnp.concatenate([jnp.where(lane < w, a, zero), jnp.where(lane >= w, a, zero)], axis=1)

    def group(i, carry):
        r0 = pl.multiple_of(i * R, R)
        rows = pl.ds(r0, R)
        lr = sm_ref[rows, SM_GLA_LR:SM_GLA_LR + GLA_GATE_RANK]
        logits = _dot(lr, wlr_ref[...], precision=HIGHEST) + blr_ref[...]
        logg = (_log_sigmoid(logits) * (1.0 / GLA_GATE_TAU)).reshape(G, CHUNK, kw)
        tri = jnp.broadcast_to(tri_ref[...][None], (G, CHUNK, CHUNK))
        b3 = _bdot_sel(tri, logg)
        sel = jnp.broadcast_to(sel_ref[...][None], (G,) + sel_ref.shape)
        refs = _bdot_sel(sel, b3)
        b = b3.reshape(R, kw)
        q = p_ref[rows, 0:kw] * dk ** -0.5
        k = p_ref[rows, kw:2 * kw]
        att = None
        for lv in range(GLA_LEVELS):
            if lv == 0:
                qt, kt = q, k
            else:
                r = refs[:, (lv - 1) * CHUNK:lv * CHUNK, :].reshape(R, kw)
                qt = q * jnp.exp(jnp.minimum(b - r, 0.0))
                kt = k * jnp.exp(jnp.minimum(r - b, 0.0))
            qp = to_problems(qt.astype(BF16), pw)
            kp = pair_block_diag(to_problems(kt.astype(BF16), pw), dk)
            a = _bdot_nt(qp, kp) * msk_ref[lv]
            att = a if att is None else att + a
        v = p_ref[rows, 2 * kw:2 * kw + N_HEADS * dv]
        vp = to_problems(v.astype(BF16), vw)
        o = _bdot(att.astype(BF16), pair_block_diag(vp, dv))

        blast = jnp.broadcast_to(b3[:, CHUNK - 1:CHUNK, :], (G, CHUNK, kw)).reshape(R, kw)
        qe = to_problems((q * jnp.exp(b)).astype(BF16), pw)
        kl = to_problems(k * jnp.exp(blast - b), pw)
        dcol = to_problems(jnp.exp(blast), pw)
        bd = bd_ref[...]
        for p in range(NP):
            st = s_ref[p]
            for g in range(G):
                n = p * G + g
                o_n = o[n] + _dot(qe[n], st.astype(BF16))
                upd = _dot(jnp.transpose(kl[n]).astype(BF16), vp[n]) * bd
                st = st * jnp.transpose(dcol[n])[:, 0:1] + upd
                for j in range(2):
                    h = 2 * p + j
                    oh = o_n[:, j * dv:(j + 1) * dv]
                    on = oh * lax.rsqrt(jnp.mean(oh * oh, axis=-1, keepdims=True) + EPS) * nw_ref[...]
                    gate = p_ref[pl.ds(r0 + g * CHUNK, CHUNK), 2 * kw + (N_HEADS + h) * dv:2 * kw + (N_HEADS + h + 1) * dv]
                    o_ref[pl.ds(r0 + g * CHUNK, CHUNK), h * dv:(h + 1) * dv] = (on * _silu(gate)).astype(BF16)
            s_ref[p] = st
        return carry

    lax.fori_loop(0, tb // R, group, 0)


def _gla_mixer(proj, small, w_lr, b_lr, norm_w, consts, B, T):
    tri, _, sel_all, masks = consts
    M = proj.shape[0]
    tb = 512
    nt = T // tb
    kw = N_HEADS * GLA_KEY_DIM
    masks2 = np.concatenate([masks, masks], axis=2)
    bd = np.kron(np.eye(2, dtype=np.float32), np.ones((GLA_KEY_DIM, HEAD_DIM), np.float32))
    const = lambda a: pl.BlockSpec(a.shape, lambda b, t: (0,) * a.ndim)
    return pl.pallas_call(
        _gla_kernel,
        grid=(B, nt),
        in_specs=[pl.BlockSpec((tb, 1536), lambda b, t: (b * nt + t, P_GLA // 1536)),
                  pl.BlockSpec((tb, LANES), lambda b, t: (b * nt + t, 0)),
                  pl.BlockSpec((GLA_GATE_RANK, kw), lambda b, t: (0, 0)),
                  pl.BlockSpec((1, kw), lambda b, t: (0, 0)),
                  pl.BlockSpec((1, HEAD_DIM), lambda b, t: (0, 0)),
                  const(tri), const(sel_all), const(masks2), const(bd)],
        out_specs=pl.BlockSpec((tb, GROUP_WIDTH), lambda b, t: (b * nt + t, 0)),
        out_shape=jax.ShapeDtypeStruct((M, GROUP_WIDTH), BF16),
        scratch_shapes=[pltpu.VMEM((N_HEADS // 2, 2 * GLA_KEY_DIM, 2 * HEAD_DIM), F32)],
        compiler_params=_cparams(2),
        name="gla_mixer",
    )(proj, small, w_lr, b_lr.reshape(1, kw), norm_w.reshape(1, HEAD_DIM),
      jnp.asarray(tri, BF16), jnp.asarray(sel_all, BF16), jnp.asarray(masks2, F32), jnp.asarray(bd, F32))


def _gdn_kernel(x_ref, g_ref, sm_ref, cw_ref, alog_ref, dtb_ref, nw_ref, tri_ref, incl_ref, lvl_ref,
                o_ref, s_ref, halo_ref, cq_ref, ku_ref, au_ref, gl_ref):
    tb = x_ref.shape[0]
    d = HEAD_DIM
    hw = N_HEADS * d
    G = GDN_GROUP
    R = G * CHUNK
    P = GDN_PAIR
    BR = P * CHUNK
    NP = N_HEADS // P

    @pl.when(pl.program_id(1) == 0)
    def _():
        s_ref[...] = jnp.zeros_like(s_ref)
        halo_ref[...] = jnp.zeros_like(halo_ref)

    def stack(a):
        w = a.shape[1] // N_HEADS
        head = lambda h: a[:, h * w:(h + 1) * w].reshape(G, CHUNK, w)
        return jnp.concatenate([jnp.concatenate([head(p * P + j) for j in range(P)], axis=1)
                                for p in range(NP)], axis=0)

    def gate_col(a, lane, last=False):
        def head(h):
            col = a[:, lane + h:lane + h + 1].reshape(G, CHUNK, 1)
            return jnp.broadcast_to(col[:, CHUNK - 1:CHUNK, :], (G, CHUNK, 1)) if last else col
        return jnp.concatenate([jnp.concatenate([head(p * P + j) for j in range(P)], axis=1)
                                for p in range(NP)], axis=0)

    incl = incl_ref[...]
    head_of_row = lax.broadcasted_iota(jnp.int32, (BR, d), 0) // CHUNK

    def block_diag(a):
        return jnp.concatenate([jnp.where(head_of_row == j, a, 0.0) for j in range(P)], axis=2)

    def prepare(i, carry):
        r0 = pl.multiple_of(i * R, R)
        rows = pl.ds(r0, R)
        prev = x_ref[pl.ds(pl.multiple_of(jnp.maximum(r0 - 8, 0), 8), 8), :]
        prev = jnp.where(i == 0, halo_ref[...], prev)
        win = jnp.concatenate([prev, x_ref[rows, :]], axis=0)
        conv = win * cw_ref[GDN_CONV - 1:GDN_CONV, :]
        for s in range(1, GDN_CONV):
            conv = conv + pltpu.roll(win, s, axis=0) * cw_ref[GDN_CONV - 1 - s:GDN_CONV - s, :]
        qkv = _silu(conv[8:, :])

        sm = sm_ref[rows, :]
        beta = jax.nn.sigmoid(sm)
        g = -jnp.exp(alog_ref[...]) * _softplus(sm + dtb_ref[...])
        gcum = _dot_sel(tri_ref[...], g)
        for j in range(G):
            gl_ref[i * G + j] = jnp.broadcast_to(jnp.exp(gcum[(j + 1) * CHUNK - 1:(j + 1) * CHUNK, :]), (8, LANES))

        qs, ks, vs = stack(qkv[:, 0:hw]), stack(qkv[:, hw:2 * hw]), stack(qkv[:, 2 * hw:3 * hw])
        qn = qs * lax.rsqrt(jnp.sum(qs * qs, axis=-1, keepdims=True) + EPS) * d ** -0.5
        kn = ks * lax.rsqrt(jnp.sum(ks * ks, axis=-1, keepdims=True) + EPS)
        bcol = gate_col(beta, SM_GDN_B)
        gcol = gate_col(gcum, SM_GDN_A)
        gmat = jnp.broadcast_to(gcol, (NP * G, BR, BR))
        grow = jnp.stack([jnp.transpose(gmat[b]) for b in range(NP * G)])
        decay = jnp.exp(jnp.minimum(gcol - grow, 0.0)) * incl
        kb = kn * bcol
        kn16 = kn.astype(BF16)
        lmat = _bdot_nt(kb.astype(BF16), kn16) * decay
        tinv = lvl_ref[GDN_LEVELS] - lmat * lvl_ref[0]
        for lv in range(1, GDN_LEVELS):
            t16 = tinv.astype(BF16)
            tl = _bdot(t16, (lmat * lvl_ref[lv]).astype(BF16))
            tinv = tinv - _bdot(tl.astype(BF16), t16)
        rhs = jnp.concatenate([vs * bcol, kb * jnp.exp(gcol)], axis=2)
        sol = _bdot(tinv.astype(BF16), rhs.astype(BF16))
        sol16 = sol.astype(BF16)
        attn = _bdot_nt(qn.astype(BF16), kn16) * decay
        aw = _bdot(attn.astype(BF16), sol16)
        qp = (qn * jnp.exp(gcol) - aw[:, :, d:]).astype(BF16)
        kdec = kn * jnp.exp(gate_col(gcum, SM_GDN_A, last=True) - gcol)
        kd = block_diag(kdec)
        for p in range(NP):
            for g_ in range(G):
                b, c = p * G + g_, i * G + g_
                kt = jnp.transpose(kd[b]).astype(BF16)
                ku = _dot(kt, sol16[b])
                for j in range(P):
                    h = p * P + j
                    ku_ref[c, h * d:(h + 1) * d, :] = ku[j * d:(j + 1) * d, :d]
                    au_ref[c, h * CHUNK:(h + 1) * CHUNK, :] = aw[b, j * CHUNK:(j + 1) * CHUNK, :d]
                    cq_ref[c * N_HEADS + h, 0:d, :] = ku[j * d:(j + 1) * d, d:].astype(BF16)
                    cq_ref[c * N_HEADS + h, d:d + CHUNK, :] = qp[b, j * CHUNK:(j + 1) * CHUNK, :]
        return carry

    lax.fori_loop(0, tb // R, prepare, 0)
    halo_ref[...] = x_ref[tb - 8:tb, :]

    def recur(c, carry):
        rows = pl.ds(pl.multiple_of(c * CHUNK, CHUNK), CHUNK)
        gl = gl_ref[c]
        for h in range(N_HEADS):
            hs = slice(h * d, (h + 1) * d)
            st = s_ref[h]
            r = _dot(cq_ref[c * N_HEADS + h], st.astype(BF16))
            s_ref[h] = st * gl[0:1, SM_GDN_A + h:SM_GDN_A + h + 1] - r[:d] + ku_ref[c, hs, :]
            oh = r[d:] + au_ref[c, h * CHUNK:(h + 1) * CHUNK, :]
            on = oh * lax.rsqrt(jnp.mean(oh * oh, axis=-1, keepdims=True) + EPS) * nw_ref[...]
            o_ref[rows, hs] = (on * _silu(g_ref[rows, hs])).astype(BF16)
        return carry

    lax.fori_loop(0, tb // CHUNK, recur, 0)


def _gdn_mixer(proj, small, conv_w, a_log, dt_bias, norm_w, consts, B, T):
    tri, _, _, masks = consts
    M = proj.shape[0]
    tb = 512
    nt = T // tb
    nc = tb // CHUNK
    hb = N_HEADS * CHUNK
    pad = lambda v: jnp.zeros((1, LANES), F32).at[0, SM_GDN_A:SM_GDN_A + N_HEADS].set(v)
    tri_g = np.kron(np.eye(GDN_GROUP, dtype=np.float32), tri)
    incl_bd = np.kron(np.eye(GDN_PAIR, dtype=np.float32), tri)
    eye_p = np.eye(GDN_PAIR, dtype=np.float32)
    lvl = np.stack([np.kron(eye_p, m) for m in masks[1:]] + [np.eye(GDN_PAIR * CHUNK, dtype=np.float32)])
    assert lvl.shape[0] == GDN_LEVELS + 1
    const = lambda a: pl.BlockSpec(a.shape, lambda b, t: (0,) * a.ndim)
    return pl.pallas_call(
        _gdn_kernel,
        grid=(B, nt),
        in_specs=[pl.BlockSpec((tb, 1536), lambda b, t: (b * nt + t, P_GDN // 1536)),
                  pl.BlockSpec((tb, GROUP_WIDTH), lambda b, t: (b * nt + t, P_GDN_G // GROUP_WIDTH)),
                  pl.BlockSpec((tb, LANES), lambda b, t: (b * nt + t, 0)),
                  pl.BlockSpec((GDN_CONV, 1536), lambda b, t: (0, 0)),
                  pl.BlockSpec((1, LANES), lambda b, t: (0, 0)),
                  pl.BlockSpec((1, LANES), lambda b, t: (0, 0)),
                  pl.BlockSpec((1, HEAD_DIM), lambda b, t: (0, 0)),
                  const(tri_g), const(incl_bd), const(lvl)],
        out_specs=pl.BlockSpec((tb, GROUP_WIDTH), lambda b, t: (b * nt + t, 0)),
        out_shape=jax.ShapeDtypeStruct((M, GROUP_WIDTH), BF16),
        scratch_shapes=[pltpu.VMEM((N_HEADS, HEAD_DIM, HEAD_DIM), F32),
                        pltpu.VMEM((8, 1536), F32),
                        pltpu.VMEM((nc * N_HEADS, HEAD_DIM + CHUNK, HEAD_DIM), BF16),
                        pltpu.VMEM((nc, N_HEADS * HEAD_DIM, HEAD_DIM), F32),
                        pltpu.VMEM((nc, hb, HEAD_DIM), F32),
                        pltpu.VMEM((nc, 8, LANES), F32)],
        compiler_params=_cparams(2),
        name="gdn_mixer",
    )(proj, proj, small, conv_w, pad(a_log), pad(dt_bias), norm_w.reshape(1, HEAD_DIM),
      jnp.asarray(tri_g, BF16), jnp.asarray(incl_bd, F32), jnp.asarray(lvl, F32))


POOL_HALO = 16


def _pool_kernel(u_ref, w_ref, sc_ref, o_ref, halo_ref):
    tb = u_ref.shape[0]
    gd = GROUP_WIDTH // len(POOL_WINDOWS)
    t = pl.program_id(1)

    @pl.when(t == 0)
    def _():
        halo_ref[...] = jnp.zeros_like(halo_ref)

    pos = (t * tb + 1 + lax.broadcasted_iota(jnp.int32, (tb, 1), 0)).astype(F32)
    for gi, win in enumerate(POOL_WINDOWS):
        cols = slice(gi * gd, (gi + 1) * gd)
        u = u_ref[:, cols]
        ssum = jnp.concatenate([halo_ref[:, cols], u], axis=0)
        shift = 1
        while shift < win:
            ssum = ssum + pltpu.roll(ssum, shift, axis=0)
            shift *= 2
        mean = ssum[POOL_HALO:, :] / jnp.minimum(pos, float(win))
        y = _dot((mean - u).astype(BF16), w_ref[gi])
        o_ref[:, cols] = (y * sc_ref[:, cols]).astype(BF16)
    halo_ref[...] = u_ref[tb - POOL_HALO:tb, :]


def _pool_mixer(proj, w_bf16, scale, B, T):
    M = proj.shape[0]
    tb = 512
    nt = T // tb
    gd = GROUP_WIDTH // len(POOL_WINDOWS)
    return pl.pallas_call(
        _pool_kernel,
        grid=(B, nt),
        in_specs=[pl.BlockSpec((tb, GROUP_WIDTH), lambda b, t: (b * nt + t, P_POOL // GROUP_WIDTH)),
                  pl.BlockSpec((len(POOL_WINDOWS), gd, gd), lambda b, t: (0, 0, 0)),
                  pl.BlockSpec((1, GROUP_WIDTH), lambda b, t: (0, 0))],
        out_specs=pl.BlockSpec((tb, GROUP_WIDTH), lambda b, t: (b * nt + t, 0)),
        out_shape=jax.ShapeDtypeStruct((M, GROUP_WIDTH), BF16),
        scratch_shapes=[pltpu.VMEM((POOL_HALO, GROUP_WIDTH), F32)],
        compiler_params=_cparams(2),
        name="pool_mixer",
    )(proj, w_bf16, scale.reshape(1, GROUP_WIDTH))


FOX_CUM_BLOCK = 512


def _fox_gate_kernel(sm_ref, bias_ref, tri_ref, o_ref, carry_ref):
    @pl.when(pl.program_id(1) == 0)
    def _():
        carry_ref[...] = jnp.zeros_like(carry_ref)

    lf = _log_sigmoid(sm_ref[...] + bias_ref[...])
    cum = _dot_sel(tri_ref[...], lf) + carry_ref[...]
    carry_ref[...] = cum[FOX_CUM_BLOCK - 1:FOX_CUM_BLOCK, :]
    o_ref[...] = cum


def _fox_gate_cumsum(small, f_bias, B, T):
    tb = FOX_CUM_BLOCK
    nt = T // tb
    tri = jnp.asarray(np.tril(np.ones((tb, tb), np.float32)), BF16)
    bias = jnp.zeros((1, LANES), F32).at[0, SM_FOX_F:SM_FOX_F + N_HEADS].set(f_bias)
    return pl.pallas_call(
        _fox_gate_kernel,
        grid=(B, nt),
        in_specs=[pl.BlockSpec((tb, LANES), lambda b, t: (b * nt + t, 0)),
                  pl.BlockSpec((1, LANES), lambda b, t: (0, 0)),
                  pl.BlockSpec((tb, tb), lambda b, t: (0, 0))],
        out_specs=pl.BlockSpec((tb, LANES), lambda b, t: (b * nt + t, 0)),
        out_shape=jax.ShapeDtypeStruct((B * T, LANES), F32),
        scratch_shapes=[pltpu.VMEM((1, LANES), F32)],
        compiler_params=_cparams(2),
        name="fox_gate_cumsum",
    )(small, bias, tri)


FOX_BLOCK = 512
FOX_KEY_BLOCK = 512


LOG2E = math.log2(math.e)


def _fox_kernel(q_ref, k_ref, v_ref, f_ref, o_ref, kb_ref, vt_ref):
    qi = pl.program_id(1)
    blk = FOX_BLOCK
    d = HEAD_DIM
    T = k_ref.shape[0]

    @pl.when(qi == 0)
    def _():
        kb_ref[...] = k_ref[...].astype(BF16)
        for h in range(N_HEADS):
            for c in range(T // blk):
                vt_ref[h, :, c * blk:(c + 1) * blk] = jnp.transpose(
                    v_ref[c * blk:(c + 1) * blk, h * d:(h + 1) * d]).astype(BF16)

    q0 = pl.multiple_of(qi * blk, blk)
    qt = [jnp.transpose(q_ref[:, h * d:(h + 1) * d] * (d ** -0.5 * LOG2E)).astype(BF16)
          for h in range(N_HEADS)]
    fcol = lambda h, start, n: f_ref[pl.ds(start, n), SM_FOX_F + h:SM_FOX_F + h + 1]
    f0 = [fcol(h, q0, 8)[0:1, :] for h in range(N_HEADS)]

    kblk = FOX_KEY_BLOCK
    ratio = blk // kblk

    def block(ki, carry, masked):
        k0 = pl.multiple_of(ki * kblk, kblk)
        scores = [_dot(kb_ref[pl.ds(k0, kblk), h * d:(h + 1) * d], qt[h]) for h in range(N_HEADS)]
        stats, probs = [], []
        for h in range(N_HEADS):
            m, l, _ = carry[h]
            s = scores[h] - (fcol(h, k0, kblk) - f0[h]) * LOG2E
            if masked:
                key = k0 + lax.broadcasted_iota(jnp.int32, (kblk, blk), 0)
                qry = q0 + lax.broadcasted_iota(jnp.int32, (kblk, blk), 1)
                s = jnp.where(key <= qry, s, -jnp.inf)
            m_new = jnp.maximum(m, jnp.max(s, axis=0, keepdims=True))
            alpha = jnp.exp2(m - m_new)
            p = jnp.exp2(s - m_new)
            stats.append((m_new, alpha, alpha * l + jnp.sum(p, axis=0, keepdims=True)))
            probs.append(p.astype(BF16))
        out = []
        for h in range(N_HEADS):
            m_new, alpha, l = stats[h]
            acc = alpha * carry[h][2] + _dot(vt_ref[h, :, pl.ds(k0, kblk)], probs[h])
            out.append((m_new, l, acc))
        return tuple(out)

    init = tuple((jnp.full((1, blk), -jnp.inf, F32), jnp.zeros((1, blk), F32), jnp.zeros((d, blk), F32))
                 for _ in range(N_HEADS))
    carry = lax.fori_loop(0, qi * ratio, lambda ki, c: block(ki, c, False), init)
    for j in range(ratio):
        carry = block(qi * ratio + j, carry, True)
    for h in range(N_HEADS):
        _, l, acc = carry[h]
        o_ref[:, h * d:(h + 1) * d] = jnp.transpose(acc / l).astype(BF16)


def _fox_attention(proj, fcum, B, T):
    M = proj.shape[0]
    blk = FOX_BLOCK
    nq = T // blk
    return pl.pallas_call(
        _fox_kernel,
        grid=(B, nq),
        in_specs=[pl.BlockSpec((blk, GROUP_WIDTH), lambda b, i: (b * nq + i, P_FOX // GROUP_WIDTH)),
                  pl.BlockSpec((T, GROUP_WIDTH), lambda b, i: (b, P_FOX // GROUP_WIDTH + 1)),
                  pl.BlockSpec((T, GROUP_WIDTH), lambda b, i: (b, P_FOX // GROUP_WIDTH + 2)),
                  pl.BlockSpec((T, LANES), lambda b, i: (b, 0))],
        out_specs=pl.BlockSpec((blk, GROUP_WIDTH), lambda b, i: (b * nq + i, 0)),
        out_shape=jax.ShapeDtypeStruct((M, GROUP_WIDTH), BF16),
        scratch_shapes=[pltpu.VMEM((T, GROUP_WIDTH), BF16),
                        pltpu.VMEM((N_HEADS, HEAD_DIM, T), BF16)],
        compiler_params=_cparams(2),
        name="fox_attention",
    )(proj, proj, proj, fcum)


def _outproj_kernel(ya_ref, yb_ref, yc_ref, yd_ref, w_ref, x_ref, mod_ref, o_ref):
    gw = GROUP_WIDTH
    acc = _dot(ya_ref[...], w_ref[0, 0:gw, :])
    acc = acc + _dot(yb_ref[...], w_ref[0, gw:2 * gw, :])
    acc = acc + _dot(yc_ref[...], w_ref[0, 2 * gw:3 * gw, :])
    acc = acc + _dot(yd_ref[...], w_ref[0, 3 * gw:4 * gw, :])
    o_ref[...] = x_ref[...] + mod_ref[0, 2:3, :] * acc


def _out_projection(ys, w_bf16, x2, mod, l, T):
    M, D = x2.shape
    tm, tn = 1024, 1024
    tpb = T // tm
    yspec = pl.BlockSpec((tm, GROUP_WIDTH), lambda i, j: (i, 0))
    return pl.pallas_call(
        _outproj_kernel,
        grid=(M // tm, D // tn),
        in_specs=[yspec, yspec, yspec, yspec,
                  pl.BlockSpec((1, D, tn), lambda i, j: (l, 0, j)),
                  pl.BlockSpec((tm, tn), lambda i, j: (i, j)),
                  pl.BlockSpec((1, 6, tn), lambda i, j: (i // tpb, 0, j))],
        out_specs=pl.BlockSpec((tm, tn), lambda i, j: (i, j)),
        out_shape=jax.ShapeDtypeStruct((M, D), F32),
        compiler_params=_cparams(2),
        name="out_projection",
    )(*ys, w_bf16, x2, mod)


def _ffn_kernel(x_ref, mod_ref, nw_ref, wg_ref, wu_ref, cw_ref, cb_ref, wd_ref, fw_ref, o_ref,
                h_ref, halo_ref, *, tiles_per_seq, final_norm):
    i = pl.program_id(0)
    j = pl.program_id(1)
    tm, tf = x_ref.shape[0], wg_ref.shape[2]

    @pl.when(j == 0)
    def _():
        x = x_ref[...]
        h = _rms_mod(x, nw_ref[...], mod_ref[0, 3:4, :], mod_ref[0, 4:5, :])
        h_ref[...] = h.astype(BF16)
        o_ref[...] = x

    h = h_ref[...]
    gate = _dot(h, wg_ref[0])
    up = _dot(h, wu_ref[0])
    prev = jnp.where(i % tiles_per_seq == 0, 0.0, halo_ref[j])
    row = lax.broadcasted_iota(jnp.int32, (tm, tf), 0)
    g1 = jnp.where(row == 0, prev[7:8, :], pltpu.roll(gate, 1, axis=0))
    g2 = jnp.where(row == 0, prev[6:7, :], jnp.where(row == 1, prev[7:8, :], pltpu.roll(gate, 2, axis=0)))
    halo_ref[j] = gate[tm - 8:tm, :]
    conv = cw_ref[0, 0:1, :] * g2 + cw_ref[0, 1:2, :] * g1 + cw_ref[0, 2:3, :] * gate + cb_ref[0]
    act = (_silu(conv) * up).astype(BF16)
    o_ref[...] += mod_ref[0, 5:6, :] * _dot(act, wd_ref[0])

    if final_norm:
        @pl.when(j == pl.num_programs(1) - 1)
        def _():
            y = o_ref[...]
            o_ref[...] = y * lax.rsqrt(jnp.mean(y * y, axis=-1, keepdims=True) + EPS) * fw_ref[...]


def _conv_ffn(x2, mod, norm_w, wg, wu, conv_w, conv_b, wd, final_w, l, T):
    M, D = x2.shape
    L, _, F = wg.shape
    tm, tf = 1024, 512
    tpb = T // tm
    return pl.pallas_call(
        functools.partial(_ffn_kernel, tiles_per_seq=tpb, final_norm=(l == L - 1)),
        grid=(M // tm, F // tf),
        in_specs=[pl.BlockSpec((tm, D), lambda i, j: (i, 0), pipeline_mode=pl.Buffered(1)),
                  pl.BlockSpec((1, 6, D), lambda i, j: (i // tpb, 0, 0)),
                  pl.BlockSpec((1, D), lambda i, j: (0, 0)),
                  pl.BlockSpec((1, D, tf), lambda i, j: (l, 0, j)),
                  pl.BlockSpec((1, D, tf), lambda i, j: (l, 0, j)),
                  pl.BlockSpec((1, FFN_CONV, tf), lambda i, j: (l, 0, j)),
                  pl.BlockSpec((1, 1, tf), lambda i, j: (l, 0, j)),
                  pl.BlockSpec((1, tf, D), lambda i, j: (l, j, 0)),
                  pl.BlockSpec((1, D), lambda i, j: (0, 0))],
        out_specs=pl.BlockSpec((tm, D), lambda i, j: (i, 0)),
        out_shape=jax.ShapeDtypeStruct((M, D), F32),
        scratch_shapes=[pltpu.VMEM((tm, D), BF16),
                        pltpu.VMEM((F // tf, 8, tf), F32)],
        compiler_params=pltpu.CompilerParams(dimension_semantics=("arbitrary", "arbitrary"),
                                             vmem_limit_bytes=FFN_VMEM_LIMIT),
        name="conv_ffn",
    )(x2, mod, norm_w.reshape(1, D), wg, wu, conv_w, conv_b.reshape(L, 1, F), wd, final_w.reshape(1, D))


def kernel(x, c, w_mod, b_mod, norm_mix, norm_ffn, w_in, gla_w_lr, gla_b_lr, gla_norm, gdn_conv, gdn_a_log, gdn_dt_bias, gdn_norm, pool_w, pool_scale, fox_f_bias, w_out, ffn_w_gate, ffn_w_up, ffn_conv_w, ffn_conv_b, ffn_w_down, norm_final):
    B, T, D = x.shape
    L = w_mod.shape[0]
    assert D == D_MODEL and T % 2048 == 0 and B <= 8
    consts = _chunk_constants()

    c_pad = jnp.zeros((8, D), F32).at[:B].set(c)
    mod_all = _modulation(c_pad, w_mod, b_mod)
    w_main, w_small = _prepare_in_weights(w_in)
    w_gate, w_up = _to_bf16(ffn_w_gate, 256), _to_bf16(ffn_w_up, 256)
    w_down, w_o = _to_bf16(ffn_w_down, 512), _to_bf16(w_out, 512)
    x2 = x.reshape(B * T, D)
    for l in range(L):
        mod = mod_all[l, :B].reshape(B, 6, D)
        proj, small = _in_projection(x2, mod, norm_mix[l], w_main, w_small, l, T)
        y_a = _gla_mixer(proj, small, gla_w_lr[l], gla_b_lr[l], gla_norm[l], consts, B, T)
        y_b = _gdn_mixer(proj, small, gdn_conv[l], gdn_a_log[l], gdn_dt_bias[l], gdn_norm[l], consts, B, T)
        y_c = _pool_mixer(proj, pool_w[l].astype(BF16), pool_scale[l], B, T)
        y_d = _fox_attention(proj, _fox_gate_cumsum(small, fox_f_bias[l], B, T), B, T)
        x2 = _out_projection((y_a, y_b, y_c, y_d), w_o, x2, mod, l, T)
        x2 = _conv_ffn(x2, mod, norm_ffn[l], w_gate, w_up, ffn_conv_w, ffn_conv_b, w_down, norm_final, l, T)
    return x2.reshape(B, T, D)
```

```python
import functools
import math

import numpy as np
import jax
import jax.numpy as jnp
from jax import lax
from jax.experimental import pallas as pl
from jax.experimental.pallas import tpu as pltpu

F32 = jnp.float32
BF16 = jnp.bfloat16
HIGHEST = lax.Precision.HIGHEST

D_MODEL = 2048
GROUP_WIDTH = D_MODEL // 4
N_HEADS = 4
HEAD_DIM = GROUP_WIDTH // 4
GLA_KEY_DIM = HEAD_DIM // 2
GLA_GATE_RANK = 16
GLA_GATE_TAU = 16.0
GDN_CONV = 4
POOL_WINDOWS = (2, 4, 8, 16)
CHUNK = 64
D_FF = 256 * int(math.ceil(8 * D_MODEL / 3 / 256))
FFN_CONV = 3
EPS = 1e-6

IN_SPLITS = (256, 256, 512, 512, 16, 1536, 512, 4, 4, 512, 512, 512, 512, 4)
IN_OFFS = tuple(int(sum(IN_SPLITS[:i])) for i in range(len(IN_SPLITS) + 1))

LANES = 128
P_FOX = 0
P_GLA = 1536
P_GDN = 3072
P_GDN_G = 4608
P_POOL = 5120
P_WIDTH = 5632
SM_GLA_LR = 0
SM_GDN_B = 16
SM_GDN_A = 20
SM_FOX_F = 28

VMEM_LIMIT = 56 * 1024 * 1024
FFN_VMEM_LIMIT = 60 * 1024 * 1024


def _cparams(n_axes):
    return pltpu.CompilerParams(dimension_semantics=("arbitrary",) * n_axes,
                                vmem_limit_bytes=VMEM_LIMIT)


def _dot(a, b, precision=None):
    return jnp.dot(a, b, preferred_element_type=F32, precision=precision)


def _dot_nt(a, b):
    return lax.dot_general(a, b, (((1,), (1,)), ((), ())), preferred_element_type=F32)


def _bdot(a, b):
    return lax.dot_general(a, b, (((2,), (1,)), ((0,), (0,))), preferred_element_type=F32)


def _bdot_nt(a, b):
    return lax.dot_general(a, b, (((2,), (2,)), ((0,), (0,))), preferred_element_type=F32)


def _dot_sel(sel_bf16, x):
    x1 = x.astype(BF16)
    r1 = x - x1.astype(F32)
    x2 = r1.astype(BF16)
    x3 = (r1 - x2.astype(F32)).astype(BF16)
    return _dot(sel_bf16, x1) + _dot(sel_bf16, x2) + _dot(sel_bf16, x3)


def _bdot_sel(sel_bf16, x):
    x1 = x.astype(BF16)
    r1 = x - x1.astype(F32)
    x2 = r1.astype(BF16)
    x3 = (r1 - x2.astype(F32)).astype(BF16)
    return _bdot(sel_bf16, x1) + _bdot(sel_bf16, x2) + _bdot(sel_bf16, x3)


def _log_sigmoid(x):
    return jnp.minimum(x, 0.0) - jnp.log1p(jnp.exp(-jnp.abs(x)))


def _softplus(x):
    return jnp.maximum(x, 0.0) + jnp.log1p(jnp.exp(-jnp.abs(x)))


def _silu(x):
    return x * jax.nn.sigmoid(x)


def _rms_mod(x, norm_w, shift, scale):
    gain = norm_w * (1.0 + scale)
    return x * lax.rsqrt(jnp.mean(x * x, axis=-1, keepdims=True) + EPS) * gain + shift


def _cast_kernel(w_ref, o_ref):
    o_ref[...] = w_ref[...].astype(BF16)


def _to_bf16(w, rows):
    L, R, C = w.shape
    spec = pl.BlockSpec((1, rows, C), lambda l, r: (l, r, 0))
    return pl.pallas_call(
        _cast_kernel,
        grid=(L, R // rows),
        in_specs=[spec],
        out_specs=spec,
        out_shape=jax.ShapeDtypeStruct(w.shape, BF16),
        compiler_params=_cparams(2),
        name="weight_to_bf16",
    )(w)


def _mod_kernel(c_ref, w_ref, b_ref, o_ref):
    cond = _silu(c_ref[...])
    c_hi = cond.astype(BF16)
    c_lo = (cond - c_hi.astype(F32)).astype(BF16)
    w = w_ref[0]
    w_hi = w.astype(BF16)
    w_lo = (w - w_hi.astype(F32)).astype(BF16)
    both = _dot(jnp.concatenate([c_hi, c_lo], axis=0), w_hi)
    o_ref[0] = both[:8] + both[8:] + _dot(c_hi, w_lo) + b_ref[0]


def _modulation(c_pad, w_mod, b_mod):
    L, D, N = w_mod.shape
    tn = 1024
    return pl.pallas_call(
        _mod_kernel,
        grid=(L, N // tn),
        in_specs=[pl.BlockSpec((8, D), lambda l, j: (0, 0)),
                  pl.BlockSpec((1, D, tn), lambda l, j: (l, 0, j)),
                  pl.BlockSpec((1, 1, tn), lambda l, j: (l, 0, j))],
        out_specs=pl.BlockSpec((1, 8, tn), lambda l, j: (l, 0, j)),
        out_shape=jax.ShapeDtypeStruct((L, 8, N), F32),
        compiler_params=_cparams(2),
        name="modulation",
    )(c_pad, w_mod, b_mod.reshape(L, 1, N))


def _wprep_kernel(w_ref, main_ref, small_ref):
    tk = w_ref.shape[2]
    nf = IN_SPLITS[13]
    assert (SM_GLA_LR, SM_GDN_B, SM_GDN_A, SM_FOX_F) == (0, 16, 20, 28) and nf == 4
    for l in range(w_ref.shape[1]):
        off = 0
        for i in (10, 11, 12, 0, 1, 2, 3, 5, 6, 9):
            width = IN_SPLITS[i]
            main_ref[l, :, off:off + width] = jnp.transpose(w_ref[IN_OFFS[i]:IN_OFFS[i + 1], l, :]).astype(BF16)
            off += width
        tail = w_ref[IN_OFFS[14] - 8:IN_OFFS[14], l, :]
        tail = jnp.where(lax.broadcasted_iota(jnp.int32, (8, tk), 0) >= 8 - nf, tail, 0.0)
        narrow = jnp.concatenate(
            [w_ref[IN_OFFS[4]:IN_OFFS[5], l, :],
             w_ref[IN_OFFS[7]:IN_OFFS[9], l, :],
             tail,
             jnp.zeros((LANES - 32, tk), F32)], axis=0)
        small_ref[l] = jnp.transpose(narrow).astype(BF16)


def _prepare_in_weights(w_in):
    L, D, N = w_in.shape
    tk = 128
    wt = jnp.transpose(w_in, (2, 0, 1))
    return pl.pallas_call(
        _wprep_kernel,
        grid=(D // tk,),
        in_specs=[pl.BlockSpec((N, L, tk), lambda r: (0, 0, r))],
        out_specs=[pl.BlockSpec((L, tk, P_WIDTH), lambda r: (0, r, 0)),
                   pl.BlockSpec((L, tk, LANES), lambda r: (0, r, 0))],
        out_shape=[jax.ShapeDtypeStruct((L, D, P_WIDTH), BF16),
                   jax.ShapeDtypeStruct((L, D, LANES), BF16)],
        compiler_params=_cparams(1),
        name="in_weight_prep",
    )(wt)


INPROJ_SLABS = 8


def _inproj_kernel(x_ref, mod_ref, nw_ref, w_ref, ws_ref, o_ref, os_ref, ha_ref, hb_ref):
    i = pl.program_id(0)
    j = pl.program_id(1)
    slab = x_ref.shape[0]

    def row(fill_ref, use_ref):
        def normalise_slab():
            row0 = pl.multiple_of(jnp.minimum(j, INPROJ_SLABS - 1) * slab, slab)
            h = _rms_mod(x_ref[...], nw_ref[...], mod_ref[0, 0:1, :], mod_ref[0, 1:2, :])
            fill_ref[pl.ds(row0, slab), :] = h.astype(BF16)

        @pl.when(i == 0)
        def _():
            normalise_slab()

        @pl.when(i > 0)
        def _():
            @pl.when(j == 0)
            def _():
                os_ref[...] = _dot(use_ref[...], ws_ref[0])

            normalise_slab()
            o_ref[...] = _dot(use_ref[...], w_ref[0])

    @pl.when(i % 2 == 0)
    def _():
        row(ha_ref, hb_ref)

    @pl.when(i % 2 == 1)
    def _():
        row(hb_ref, ha_ref)


def _in_projection(x2, mod, norm_w, w_main, w_small, l, T):
    M, D = x2.shape
    N = w_main.shape[2]
    tm, tn = 2048, 512
    tpb = T // tm
    n_tiles = M // tm
    slab = tm // INPROJ_SLABS
    assert N // tn >= INPROJ_SLABS
    nxt = lambda i: jnp.minimum(i, n_tiles - 1)
    cur = lambda i: jnp.maximum(i - 1, 0)
    return pl.pallas_call(
        _inproj_kernel,
        grid=(n_tiles + 1, N // tn),
        in_specs=[pl.BlockSpec((slab, D), lambda i, j: (nxt(i) * INPROJ_SLABS + jnp.minimum(j, INPROJ_SLABS - 1), 0)),
                  pl.BlockSpec((1, 6, D), lambda i, j: (nxt(i) // tpb, 0, 0)),
                  pl.BlockSpec((1, D), lambda i, j: (0, 0)),
                  pl.BlockSpec((1, D, tn), lambda i, j: (l, 0, j)),
                  pl.BlockSpec((1, D, LANES), lambda i, j: (l, 0, 0))],
        out_specs=[pl.BlockSpec((tm, tn), lambda i, j: (cur(i), jnp.where(i > 0, j, 0))),
                   pl.BlockSpec((tm, LANES), lambda i, j: (cur(i), 0))],
        out_shape=[jax.ShapeDtypeStruct((M, N), F32),
                   jax.ShapeDtypeStruct((M, LANES), F32)],
        scratch_shapes=[pltpu.VMEM((tm, D), BF16), pltpu.VMEM((tm, D), BF16)],
        compiler_params=_cparams(2),
        name="in_projection",
    )(x2, mod, norm_w.reshape(1, D), w_main, w_small)


GLA_LEVELS = 7
GLA_GROUP = 8
GDN_GROUP = 8
GDN_PAIR = 2
GDN_LEVELS = 6


def _chunk_constants():
    i = np.arange(CHUNK)[:, None]
    j = np.arange(CHUNK)[None, :]
    tri = (i >= j).astype(np.float32)
    strict = (i > j).astype(np.float32)
    sels, masks = [], [(i == j).astype(np.float32)]
    for lv in range(1, GLA_LEVELS):
        m = 1 << (lv - 1)
        ref_row = (np.arange(CHUNK) // (2 * m)) * 2 * m + m
        sels.append((j == ref_row[:, None]).astype(np.float32))
        same = (i // (2 * m)) == (j // (2 * m))
        masks.append((same & ((i % (2 * m)) >= m) & ((j % (2 * m)) < m)).astype(np.float32))
    sel_all = np.concatenate(sels, axis=0)
    return tri, strict, sel_all, np.stack(masks)


def _gla_kernel(p_ref, sm_ref, wlr_ref, blr_ref, nw_ref, tri_ref, sel_ref, msk_ref, bd_ref, o_ref, s_ref):
    tb = p_ref.shape[0]
    dk, dv = GLA_KEY_DIM, HEAD_DIM
    kw = N_HEADS * dk
    G = GLA_GROUP
    R = G * CHUNK
    NP = N_HEADS // 2
    NB = NP * G
    pw = 2 * dk
    vw = 2 * dv

    @pl.when(pl.program_id(1) == 0)
    def _():
        s_ref[...] = jnp.zeros_like(s_ref)

    def to_problems(a, w):
        return jnp.concatenate([a[:, p * w:(p + 1) * w].reshape(G, CHUNK, w) for p in range(NP)], axis=0)

    def pair_block_diag(a, w):
        lane = lax.broadcasted_iota(jnp.int32, a.shape, 2)
        zero = jnp.zeros_like(a)
        return jnp.concatenate([jnp.where(lane < w, a, zero), jnp.where(lane >= w, a, zero)], axis=1)

    def group(i, carry):
        r0 = pl.multiple_of(i * R, R)
        rows = pl.ds(r0, R)
        lr = sm_ref[rows, SM_GLA_LR:SM_GLA_LR + GLA_GATE_RANK]
        logits = _dot(lr, wlr_ref[...], precision=HIGHEST) + blr_ref[...]
        logg = (_log_sigmoid(logits) * (1.0 / GLA_GATE_TAU)).reshape(G, CHUNK, kw)
        tri = jnp.broadcast_to(tri_ref[...][None], (G, CHUNK, CHUNK))
        b3 = _bdot_sel(tri, logg)
        sel = jnp.broadcast_to(sel_ref[...][None], (G,) + sel_ref.shape)
        refs = _bdot_sel(sel, b3)
        b = b3.reshape(R, kw)
        q = p_ref[rows, 0:kw] * dk ** -0.5
        k = p_ref[rows, kw:2 * kw]
        att = None
        for lv in range(GLA_LEVELS):
            if lv == 0:
                qt, kt = q, k
            else:
                r = refs[:, (lv - 1) * CHUNK:lv * CHUNK, :].reshape(R, kw)
                qt = q * jnp.exp(jnp.minimum(b - r, 0.0))
                kt = k * jnp.exp(jnp.minimum(r - b, 0.0))
            qp = to_problems(qt.astype(BF16), pw)
            kp = pair_block_diag(to_problems(kt.astype(BF16), pw), dk)
            a = _bdot_nt(qp, kp) * msk_ref[lv]
            att = a if att is None else att + a
        v = p_ref[rows, 2 * kw:2 * kw + N_HEADS * dv]
        vp = to_problems(v.astype(BF16), vw)
        o = _bdot(att.astype(BF16), pair_block_diag(vp, dv))

        blast = jnp.broadcast_to(b3[:, CHUNK - 1:CHUNK, :], (G, CHUNK, kw)).reshape(R, kw)
        qe = to_problems((q * jnp.exp(b)).astype(BF16), pw)
        kl = to_problems(k * jnp.exp(blast - b), pw)
        dcol = to_problems(jnp.exp(blast), pw)
        bd = bd_ref[...]
        for p in range(NP):
            st = s_ref[p]
            for g in range(G):
                n = p * G + g
                o_n = o[n] + _dot(qe[n], st.astype(BF16))
                upd = _dot(jnp.transpose(kl[n]).astype(BF16), vp[n]) * bd
                st = st * jnp.transpose(dcol[n])[:, 0:1] + upd
                for j in range(2):
                    h = 2 * p + j
                    oh = o_n[:, j * dv:(j + 1) * dv]
                    on = oh * lax.rsqrt(jnp.mean(oh * oh, axis=-1, keepdims=True) + EPS) * nw_ref[...]
                    gate = p_ref[pl.ds(r0 + g * CHUNK, CHUNK), 2 * kw + (N_HEADS + h) * dv:2 * kw + (N_HEADS + h + 1) * dv]
                    o_ref[pl.ds(r0 + g * CHUNK, CHUNK), h * dv:(h + 1) * dv] = (on * _silu(gate)).astype(BF16)
            s_ref[p] = st
        return carry

    lax.fori_loop(0, tb // R, group, 0)


def _gla_mixer(proj, small, w_lr, b_lr, norm_w, consts, B, T):
    tri, _, sel_all, masks = consts
    M = proj.shape[0]
    tb = 512
    nt = T // tb
    kw = N_HEADS * GLA_KEY_DIM
    masks2 = np.concatenate([masks, masks], axis=2)
    bd = np.kron(np.eye(2, dtype=np.float32), np.ones((GLA_KEY_DIM, HEAD_DIM), np.float32))
    const = lambda a: pl.BlockSpec(a.shape, lambda b, t: (0,) * a.ndim)
    return pl.pallas_call(
        _gla_kernel,
        grid=(B, nt),
        in_specs=[pl.BlockSpec((tb, 1536), lambda b, t: (b * nt + t, P_GLA // 1536)),
                  pl.BlockSpec((tb, LANES), lambda b, t: (b * nt + t, 0)),
                  pl.BlockSpec((GLA_GATE_RANK, kw), lambda b, t: (0, 0)),
                  pl.BlockSpec((1, kw), lambda b, t: (0, 0)),
                  pl.BlockSpec((1, HEAD_DIM), lambda b, t: (0, 0)),
                  const(tri), const(sel_all), const(masks2), const(bd)],
        out_specs=pl.BlockSpec((tb, GROUP_WIDTH), lambda b, t: (b * nt + t, 0)),
        out_shape=jax.ShapeDtypeStruct((M, GROUP_WIDTH), BF16),
        scratch_shapes=[pltpu.VMEM((N_HEADS // 2, 2 * GLA_KEY_DIM, 2 * HEAD_DIM), F32)],
        compiler_params=_cparams(2),
        name="gla_mixer",
    )(proj, small, w_lr, b_lr.reshape(1, kw), norm_w.reshape(1, HEAD_DIM),
      jnp.asarray(tri, BF16), jnp.asarray(sel_all, BF16), jnp.asarray(masks2, F32), jnp.asarray(bd, F32))


def _gdn_kernel(x_ref, g_ref, sm_ref, cw_ref, alog_ref, dtb_ref, nw_ref, tri_ref, incl_ref, lvl_ref,
                o_ref, s_ref, halo_ref, cq_ref, ku_ref, au_ref, gl_ref):
    tb = x_ref.shape[0]
    d = HEAD_DIM
    hw = N_HEADS * d
    G = GDN_GROUP
    R = G * CHUNK
    P = GDN_PAIR
    BR = P * CHUNK
    NP = N_HEADS // P

    @pl.when(pl.program_id(1) == 0)
    def _():
        s_ref[...] = jnp.zeros_like(s_ref)
        halo_ref[...] = jnp.zeros_like(halo_ref)

    def stack(a):
        w = a.shape[1] // N_HEADS
        head = lambda h: a[:, h * w:(h + 1) * w].reshape(G, CHUNK, w)
        return jnp.concatenate([jnp.concatenate([head(p * P + j) for j in range(P)], axis=1)
                                for p in range(NP)], axis=0)

    def gate_col(a, lane, last=False):
        def head(h):
            col = a[:, lane + h:lane + h + 1].reshape(G, CHUNK, 1)
            return jnp.broadcast_to(col[:, CHUNK - 1:CHUNK, :], (G, CHUNK, 1)) if last else col
        return jnp.concatenate([jnp.concatenate([head(p * P + j) for j in range(P)], axis=1)
                                for p in range(NP)], axis=0)

    incl = incl_ref[...]
    head_of_row = lax.broadcasted_iota(jnp.int32, (BR, d), 0) // CHUNK

    def block_diag(a):
        return jnp.concatenate([jnp.where(head_of_row == j, a, 0.0) for j in range(P)], axis=2)

    def prepare(i, carry):
        r0 = pl.multiple_of(i * R, R)
        rows = pl.ds(r0, R)
        prev = x_ref[pl.ds(pl.multiple_of(jnp.maximum(r0 - 8, 0), 8), 8), :]
        prev = jnp.where(i == 0, halo_ref[...], prev)
        win = jnp.concatenate([prev, x_ref[rows, :]], axis=0)
        conv = win * cw_ref[GDN_CONV - 1:GDN_CONV, :]
        for s in range(1, GDN_CONV):
            conv = conv + pltpu.roll(win, s, axis=0) * cw_ref[GDN_CONV - 1 - s:GDN_CONV - s, :]
        qkv = _silu(conv[8:, :])

        sm = sm_ref[rows, :]
        beta = jax.nn.sigmoid(sm)
        g = -jnp.exp(alog_ref[...]) * _softplus(sm + dtb_ref[...])
        gcum = _dot_sel(tri_ref[...], g)
        for j in range(G):
            gl_ref[i * G + j] = jnp.broadcast_to(jnp.exp(gcum[(j + 1) * CHUNK - 1:(j + 1) * CHUNK, :]), (8, LANES))

        qs, ks, vs = stack(qkv[:, 0:hw]), stack(qkv[:, hw:2 * hw]), stack(qkv[:, 2 * hw:3 * hw])
        qn = qs * lax.rsqrt(jnp.sum(qs * qs, axis=-1, keepdims=True) + EPS) * d ** -0.5
        kn = ks * lax.rsqrt(jnp.sum(ks * ks, axis=-1, keepdims=True) + EPS)
        bcol = gate_col(beta, SM_GDN_B)
        gcol = gate_col(gcum, SM_GDN_A)
        gmat = jnp.broadcast_to(gcol, (NP * G, BR, BR))
        grow = jnp.stack([jnp.transpose(gmat[b]) for b in range(NP * G)])
        decay = jnp.exp(jnp.minimum(gcol - grow, 0.0)) * incl
        kb = kn * bcol
        kn16 = kn.astype(BF16)
        lmat = _bdot_nt(kb.astype(BF16), kn16) * decay
        tinv = lvl_ref[GDN_LEVELS] - lmat * lvl_ref[0]
        for lv in range(1, GDN_LEVELS):
            t16 = tinv.astype(BF16)
            tl = _bdot(t16, (lmat * lvl_ref[lv]).astype(BF16))
            tinv = tinv - _bdot(tl.astype(BF16), t16)
        rhs = jnp.concatenate([vs * bcol, kb * jnp.exp(gcol)], axis=2)
        sol = _bdot(tinv.astype(BF16), rhs.astype(BF16))
        sol16 = sol.astype(BF16)
        attn = _bdot_nt(qn.astype(BF16), kn16) * decay
        aw = _bdot(attn.astype(BF16), sol16)
        qp = (qn * jnp.exp(gcol) - aw[:, :, d:]).astype(BF16)
        kdec = kn * jnp.exp(gate_col(gcum, SM_GDN_A, last=True) - gcol)
        kd = block_diag(kdec)
        for p in range(NP):
            for g_ in range(G):
                b, c = p * G + g_, i * G + g_
                kt = jnp.transpose(kd[b]).astype(BF16)
                ku = _dot(kt, sol16[b])
                for j in range(P):
                    h = p * P + j
                    ku_ref[c, h * d:(h + 1) * d, :] = ku[j * d:(j + 1) * d, :d]
                    au_ref[c, h * CHUNK:(h + 1) * CHUNK, :] = aw[b, j * CHUNK:(j + 1) * CHUNK, :d]
                    cq_ref[c * N_HEADS + h, 0:d, :] = ku[j * d:(j + 1) * d, d:].astype(BF16)
                    cq_ref[c * N_HEADS + h, d:d + CHUNK, :] = qp[b, j * CHUNK:(j + 1) * CHUNK, :]
        return carry

    lax.fori_loop(0, tb // R, prepare, 0)
    halo_ref[...] = x_ref[tb - 8:tb, :]

    def recur(c, carry):
        rows = pl.ds(pl.multiple_of(c * CHUNK, CHUNK), CHUNK)
        gl = gl_ref[c]
        for h in range(N_HEADS):
            hs = slice(h * d, (h + 1) * d)
            st = s_ref[h]
            r = _dot(cq_ref[c * N_HEADS + h], st.astype(BF16))
            s_ref[h] = st * gl[0:1, SM_GDN_A + h:SM_GDN_A + h + 1] - r[:d] + ku_ref[c, hs, :]
            oh = r[d:] + au_ref[c, h * CHUNK:(h + 1) * CHUNK, :]
            on = oh * lax.rsqrt(jnp.mean(oh * oh, axis=-1, keepdims=True) + EPS) * nw_ref[...]
            o_ref[rows, hs] = (on * _silu(g_ref[rows, hs])).astype(BF16)
        return carry

    lax.fori_loop(0, tb // CHUNK, recur, 0)


def _gdn_mixer(proj, small, conv_w, a_log, dt_bias, norm_w, consts, B, T):
    tri, _, _, masks = consts
    M = proj.shape[0]
    tb = 512
    nt = T // tb
    nc = tb // CHUNK
    hb = N_HEADS * CHUNK
    pad = lambda v: jnp.zeros((1, LANES), F32).at[0, SM_GDN_A:SM_GDN_A + N_HEADS].set(v)
    tri_g = np.kron(np.eye(GDN_GROUP, dtype=np.float32), tri)
    incl_bd = np.kron(np.eye(GDN_PAIR, dtype=np.float32), tri)
    eye_p = np.eye(GDN_PAIR, dtype=np.float32)
    lvl = np.stack([np.kron(eye_p, m) for m in masks[1:]] + [np.eye(GDN_PAIR * CHUNK, dtype=np.float32)])
    assert lvl.shape[0] == GDN_LEVELS + 1
    const = lambda a: pl.BlockSpec(a.shape, lambda b, t: (0,) * a.ndim)
    return pl.pallas_call(
        _gdn_kernel,
        grid=(B, nt),
        in_specs=[pl.BlockSpec((tb, 1536), lambda b, t: (b * nt + t, P_GDN // 1536)),
                  pl.BlockSpec((tb, GROUP_WIDTH), lambda b, t: (b * nt + t, P_GDN_G // GROUP_WIDTH)),
                  pl.BlockSpec((tb, LANES), lambda b, t: (b * nt + t, 0)),
                  pl.BlockSpec((GDN_CONV, 1536), lambda b, t: (0, 0)),
                  pl.BlockSpec((1, LANES), lambda b, t: (0, 0)),
                  pl.BlockSpec((1, LANES), lambda b, t: (0, 0)),
                  pl.BlockSpec((1, HEAD_DIM), lambda b, t: (0, 0)),
                  const(tri_g), const(incl_bd), const(lvl)],
        out_specs=pl.BlockSpec((tb, GROUP_WIDTH), lambda b, t: (b * nt + t, 0)),
        out_shape=jax.ShapeDtypeStruct((M, GROUP_WIDTH), BF16),
        scratch_shapes=[pltpu.VMEM((N_HEADS, HEAD_DIM, HEAD_DIM), F32),
                        pltpu.VMEM((8, 1536), F32),
                        pltpu.VMEM((nc * N_HEADS, HEAD_DIM + CHUNK, HEAD_DIM), BF16),
                        pltpu.VMEM((nc, N_HEADS * HEAD_DIM, HEAD_DIM), F32),
                        pltpu.VMEM((nc, hb, HEAD_DIM), F32),
                        pltpu.VMEM((nc, 8, LANES), F32)],
        compiler_params=_cparams(2),
        name="gdn_mixer",
    )(proj, proj, small, conv_w, pad(a_log), pad(dt_bias), norm_w.reshape(1, HEAD_DIM),
      jnp.asarray(tri_g, BF16), jnp.asarray(incl_bd, F32), jnp.asarray(lvl, F32))


POOL_HALO = 16


def _pool_kernel(u_ref, w_ref, sc_ref, o_ref, halo_ref):
    tb = u_ref.shape[0]
    gd = GROUP_WIDTH // len(POOL_WINDOWS)
    t = pl.program_id(1)

    @pl.when(t == 0)
    def _():
        halo_ref[...] = jnp.zeros_like(halo_ref)

    pos = (t * tb + 1 + lax.broadcasted_iota(jnp.int32, (tb, 1), 0)).astype(F32)
    for gi, win in enumerate(POOL_WINDOWS):
        cols = slice(gi * gd, (gi + 1) * gd)
        u = u_ref[:, cols]
        ssum = jnp.concatenate([halo_ref[:, cols], u], axis=0)
        shift = 1
        while shift < win:
            ssum = ssum + pltpu.roll(ssum, shift, axis=0)
            shift *= 2
        mean = ssum[POOL_HALO:, :] / jnp.minimum(pos, float(win))
        y = _dot((mean - u).astype(BF16), w_ref[gi])
        o_ref[:, cols] = (y * sc_ref[:, cols]).astype(BF16)
    halo_ref[...] = u_ref[tb - POOL_HALO:tb, :]


def _pool_mixer(proj, w_bf16, scale, B, T):
    M = proj.shape[0]
    tb = 1024
    nt = T // tb
    gd = GROUP_WIDTH // len(POOL_WINDOWS)
    return pl.pallas_call(
        _pool_kernel,
        grid=(B, nt),
        in_specs=[pl.BlockSpec((tb, GROUP_WIDTH), lambda b, t: (b * nt + t, P_POOL // GROUP_WIDTH)),
                  pl.BlockSpec((len(POOL_WINDOWS), gd, gd), lambda b, t: (0, 0, 0)),
                  pl.BlockSpec((1, GROUP_WIDTH), lambda b, t: (0, 0))],
        out_specs=pl.BlockSpec((tb, GROUP_WIDTH), lambda b, t: (b * nt + t, 0)),
        out_shape=jax.ShapeDtypeStruct((M, GROUP_WIDTH), BF16),
        scratch_shapes=[pltpu.VMEM((POOL_HALO, GROUP_WIDTH), F32)],
        compiler_params=_cparams(2),
        name="pool_mixer",
    )(proj, w_bf16, scale.reshape(1, GROUP_WIDTH))


FOX_CUM_BLOCK = 1024


def _fox_gate_kernel(sm_ref, bias_ref, tri_ref, o_ref, carry_ref):
    @pl.when(pl.program_id(1) == 0)
    def _():
        carry_ref[...] = jnp.zeros_like(carry_ref)

    lf = _log_sigmoid(sm_ref[...] + bias_ref[...])
    cum = _dot_sel(tri_ref[...], lf) + carry_ref[...]
    carry_ref[...] = cum[FOX_CUM_BLOCK - 1:FOX_CUM_BLOCK, :]
    o_ref[...] = cum


def _fox_gate_cumsum(small, f_bias, B, T):
    tb = FOX_CUM_BLOCK
    nt = T // tb
    tri = jnp.asarray(np.tril(np.ones((tb, tb), np.float32)), BF16)
    bias = jnp.zeros((1, LANES), F32).at[0, SM_FOX_F:SM_FOX_F + N_HEADS].set(f_bias)
    return pl.pallas_call(
        _fox_gate_kernel,
        grid=(B, nt),
        in_specs=[pl.BlockSpec((tb, LANES), lambda b, t: (b * nt + t, 0)),
                  pl.BlockSpec((1, LANES), lambda b, t: (0, 0)),
                  pl.BlockSpec((tb, tb), lambda b, t: (0, 0))],
        out_specs=pl.BlockSpec((tb, LANES), lambda b, t: (b * nt + t, 0)),
        out_shape=jax.ShapeDtypeStruct((B * T, LANES), F32),
        scratch_shapes=[pltpu.VMEM((1, LANES), F32)],
        compiler_params=_cparams(2),
        name="fox_gate_cumsum",
    )(small, bias, tri)


FOX_BLOCK = 512
FOX_KEY_BLOCK = 512


LOG2E = math.log2(math.e)


def _fox_kernel(q_ref, k_ref, v_ref, f_ref, o_ref, kb_ref, vt_ref):
    qi = pl.program_id(1)
    blk = FOX_BLOCK
    d = HEAD_DIM
    T = k_ref.shape[0]

    @pl.when(qi == 0)
    def _():
        kb_ref[...] = k_ref[...].astype(BF16)
        for h in range(N_HEADS):
            for c in range(T // blk):
                vt_ref[h, :, c * blk:(c + 1) * blk] = jnp.transpose(
                    v_ref[c * blk:(c + 1) * blk, h * d:(h + 1) * d]).astype(BF16)

    q0 = pl.multiple_of(qi * blk, blk)
    qt = [jnp.transpose(q_ref[:, h * d:(h + 1) * d] * (d ** -0.5 * LOG2E)).astype(BF16)
          for h in range(N_HEADS)]
    fcol = lambda h, start, n: f_ref[pl.ds(start, n), SM_FOX_F + h:SM_FOX_F + h + 1]
    f0 = [fcol(h, q0, 8)[0:1, :] for h in range(N_HEADS)]

    kblk = FOX_KEY_BLOCK
    ratio = blk // kblk

    def block(ki, carry, masked):
        k0 = pl.multiple_of(ki * kblk, kblk)
        scores = [_dot(kb_ref[pl.ds(k0, kblk), h * d:(h + 1) * d], qt[h]) for h in range(N_HEADS)]
        stats, probs = [], []
        for h in range(N_HEADS):
            m, l, _ = carry[h]
            s = scores[h] - (fcol(h, k0, kblk) - f0[h]) * LOG2E
            if masked:
                key = k0 + lax.broadcasted_iota(jnp.int32, (kblk, blk), 0)
                qry = q0 + lax.broadcasted_iota(jnp.int32, (kblk, blk), 1)
                s = jnp.where(key <= qry, s, -jnp.inf)
            m_new = jnp.maximum(m, jnp.max(s, axis=0, keepdims=True))
            alpha = jnp.exp2(m - m_new)
            p = jnp.exp2(s - m_new)
            stats.append((m_new, alpha, alpha * l + jnp.sum(p, axis=0, keepdims=True)))
            probs.append(p.astype(BF16))
        out = []
        for h in range(N_HEADS):
            m_new, alpha, l = stats[h]
            acc = alpha * carry[h][2] + _dot(vt_ref[h, :, pl.ds(k0, kblk)], probs[h])
            out.append((m_new, l, acc))
        return tuple(out)

    init = tuple((jnp.full((1, blk), -jnp.inf, F32), jnp.zeros((1, blk), F32), jnp.zeros((d, blk), F32))
                 for _ in range(N_HEADS))
    carry = lax.fori_loop(0, qi * ratio, lambda ki, c: block(ki, c, False), init)
    for j in range(ratio):
        carry = block(qi * ratio + j, carry, True)
    for h in range(N_HEADS):
        _, l, acc = carry[h]
        o_ref[:, h * d:(h + 1) * d] = jnp.transpose(acc / l).astype(BF16)


def _fox_attention(proj, fcum, B, T):
    M = proj.shape[0]
    blk = FOX_BLOCK
    nq = T // blk
    return pl.pallas_call(
        _fox_kernel,
        grid=(B, nq),
        in_specs=[pl.BlockSpec((blk, GROUP_WIDTH), lambda b, i: (b * nq + i, P_FOX // GROUP_WIDTH)),
                  pl.BlockSpec((T, GROUP_WIDTH), lambda b, i: (b, P_FOX // GROUP_WIDTH + 1)),
                  pl.BlockSpec((T, GROUP_WIDTH), lambda b, i: (b, P_FOX // GROUP_WIDTH + 2)),
                  pl.BlockSpec((T, LANES), lambda b, i: (b, 0))],
        out_specs=pl.BlockSpec((blk, GROUP_WIDTH), lambda b, i: (b * nq + i, 0)),
        out_shape=jax.ShapeDtypeStruct((M, GROUP_WIDTH), BF16),
        scratch_shapes=[pltpu.VMEM((T, GROUP_WIDTH), BF16),
                        pltpu.VMEM((N_HEADS, HEAD_DIM, T), BF16)],
        compiler_params=_cparams(2),
        name="fox_attention",
    )(proj, proj, proj, fcum)


def _outproj_kernel(ya_ref, yb_ref, yc_ref, yd_ref, w_ref, x_ref, mod_ref, o_ref):
    gw = GROUP_WIDTH
    acc = _dot(ya_ref[...], w_ref[0, 0:gw, :])
    acc = acc + _dot(yb_ref[...], w_ref[0, gw:2 * gw, :])
    acc = acc + _dot(yc_ref[...], w_ref[0, 2 * gw:3 * gw, :])
    acc = acc + _dot(yd_ref[...], w_ref[0, 3 * gw:4 * gw, :])
    o_ref[...] = x_ref[...] + mod_ref[0, 2:3, :] * acc


def _out_projection(ys, w_bf16, x2, mod, l, T):
    M, D = x2.shape
    tm, tn = 1024, 1024
    tpb = T // tm
    yspec = pl.BlockSpec((tm, GROUP_WIDTH), lambda i, j: (i, 0))
    return pl.pallas_call(
        _outproj_kernel,
        grid=(M // tm, D // tn),
        in_specs=[yspec, yspec, yspec, yspec,
                  pl.BlockSpec((1, D, tn), lambda i, j: (l, 0, j)),
                  pl.BlockSpec((tm, tn), lambda i, j: (i, j)),
                  pl.BlockSpec((1, 6, tn), lambda i, j: (i // tpb, 0, j))],
        out_specs=pl.BlockSpec((tm, tn), lambda i, j: (i, j)),
        out_shape=jax.ShapeDtypeStruct((M, D), F32),
        compiler_params=_cparams(2),
        name="out_projection",
    )(*ys, w_bf16, x2, mod)


def _ffn_kernel(x_ref, mod_ref, nw_ref, wg_ref, wu_ref, cw_ref, cb_ref, wd_ref, fw_ref, o_ref,
                h_ref, halo_ref, *, tiles_per_seq, final_norm):
    i = pl.program_id(0)
    j = pl.program_id(1)
    tm, tf = x_ref.shape[0], wg_ref.shape[2]

    @pl.when(j == 0)
    def _():
        x = x_ref[...]
        h = _rms_mod(x, nw_ref[...], mod_ref[0, 3:4, :], mod_ref[0, 4:5, :])
        h_ref[...] = h.astype(BF16)
        o_ref[...] = x

    h = h_ref[...]
    gate = _dot(h, wg_ref[0])
    up = _dot(h, wu_ref[0])
    prev = jnp.where(i % tiles_per_seq == 0, 0.0, halo_ref[j])
    row = lax.broadcasted_iota(jnp.int32, (tm, tf), 0)
    g1 = jnp.where(row == 0, prev[7:8, :], pltpu.roll(gate, 1, axis=0))
    g2 = jnp.where(row == 0, prev[6:7, :], jnp.where(row == 1, prev[7:8, :], pltpu.roll(gate, 2, axis=0)))
    halo_ref[j] = gate[tm - 8:tm, :]
    conv = cw_ref[0, 0:1, :] * g2 + cw_ref[0, 1:2, :] * g1 + cw_ref[0, 2:3, :] * gate + cb_ref[0]
    act = (_silu(conv) * up).astype(BF16)
    o_ref[...] += mod_ref[0, 5:6, :] * _dot(act, wd_ref[0])

    if final_norm:
        @pl.when(j == pl.num_programs(1) - 1)
        def _():
            y = o_ref[...]
            o_ref[...] = y * lax.rsqrt(jnp.mean(y * y, axis=-1, keepdims=True) + EPS) * fw_ref[...]


def _conv_ffn(x2, mod, norm_w, wg, wu, conv_w, conv_b, wd, final_w, l, T):
    M, D = x2.shape
    L, _, F = wg.shape
    tm, tf = 1024, 512
    tpb = T // tm
    return pl.pallas_call(
        functools.partial(_ffn_kernel, tiles_per_seq=tpb, final_norm=(l == L - 1)),
        grid=(M // tm, F // tf),
        in_specs=[pl.BlockSpec((tm, D), lambda i, j: (i, 0), pipeline_mode=pl.Buffered(1)),
                  pl.BlockSpec((1, 6, D), lambda i, j: (i // tpb, 0, 0)),
                  pl.BlockSpec((1, D), lambda i, j: (0, 0)),
                  pl.BlockSpec((1, D, tf), lambda i, j: (l, 0, j)),
                  pl.BlockSpec((1, D, tf), lambda i, j: (l, 0, j)),
                  pl.BlockSpec((1, FFN_CONV, tf), lambda i, j: (l, 0, j)),
                  pl.BlockSpec((1, 1, tf), lambda i, j: (l, 0, j)),
                  pl.BlockSpec((1, tf, D), lambda i, j: (l, j, 0)),
                  pl.BlockSpec((1, D), lambda i, j: (0, 0))],
        out_specs=pl.BlockSpec((tm, D), lambda i, j: (i, 0)),
        out_shape=jax.ShapeDtypeStruct((M, D), F32),
        scratch_shapes=[pltpu.VMEM((tm, D), BF16),
                        pltpu.VMEM((F // tf, 8, tf), F32)],
        compiler_params=pltpu.CompilerParams(dimension_semantics=("arbitrary", "arbitrary"),
                                             vmem_limit_bytes=FFN_VMEM_LIMIT),
        name="conv_ffn",
    )(x2, mod, norm_w.reshape(1, D), wg, wu, conv_w, conv_b.reshape(L, 1, F), wd, final_w.reshape(1, D))


def kernel(x, c, w_mod, b_mod, norm_mix, norm_ffn, w_in, gla_w_lr, gla_b_lr, gla_norm, gdn_conv, gdn_a_log, gdn_dt_bias, gdn_norm, pool_w, pool_scale, fox_f_bias, w_out, ffn_w_gate, ffn_w_up, ffn_conv_w, ffn_conv_b, ffn_w_down, norm_final):
    B, T, D = x.shape
    L = w_mod.shape[0]
    assert D == D_MODEL and T % 2048 == 0 and B <= 8
    consts = _chunk_constants()

    c_pad = jnp.zeros((8, D), F32).at[:B].set(c)
    mod_all = _modulation(c_pad, w_mod, b_mod)
    w_main, w_small = _prepare_in_weights(w_in)
    w_gate, w_up = _to_bf16(ffn_w_gate, 256), _to_bf16(ffn_w_up, 256)
    w_down, w_o = _to_bf16(ffn_w_down, 512), _to_bf16(w_out, 512)
    x2 = x.reshape(B * T, D)
    for l in range(L):
        mod = mod_all[l, :B].reshape(B, 6, D)
        proj, small = _in_projection(x2, mod, norm_mix[l], w_main, w_small, l, T)
        y_a = _gla_mixer(proj, small, gla_w_lr[l], gla_b_lr[l], gla_norm[l], consts, B, T)
        y_b = _gdn_mixer(proj, small, gdn_conv[l], gdn_a_log[l], gdn_dt_bias[l], gdn_norm[l], consts, B, T)
        y_c = _pool_mixer(proj, pool_w[l].astype(BF16), pool_scale[l], B, T)
        y_d = _fox_attention(proj, _fox_gate_cumsum(small, fox_f_bias[l], B, T), B, T)
        x2 = _out_projection((y_a, y_b, y_c, y_d), w_o, x2, mod, l, T)
        x2 = _conv_ffn(x2, mod, norm_ffn[l], w_gate, w_up, ffn_conv_w, ffn_conv_b, w_down, norm_final, l, T)
    return x2.reshape(B, T, D)
```

```python
import functools
import math

import numpy as np
import jax
import jax.numpy as jnp
from jax import lax
from jax.experimental import pallas as pl
from jax.experimental.pallas import tpu as pltpu

F32 = jnp.float32
BF16 = jnp.bfloat16
HIGHEST = lax.Precision.HIGHEST

D_MODEL = 2048
GROUP_WIDTH = D_MODEL // 4
N_HEADS = 4
HEAD_DIM = GROUP_WIDTH // 4
GLA_KEY_DIM = HEAD_DIM // 2
GLA_GATE_RANK = 16
GLA_GATE_TAU = 16.0
GDN_CONV = 4
POOL_WINDOWS = (2, 4, 8, 16)
CHUNK = 64
D_FF = 256 * int(math.ceil(8 * D_MODEL / 3 / 256))
FFN_CONV = 3
EPS = 1e-6

IN_SPLITS = (256, 256, 512, 512, 16, 1536, 512, 4, 4, 512, 512, 512, 512, 4)
IN_OFFS = tuple(int(sum(IN_SPLITS[:i])) for i in range(len(IN_SPLITS) + 1))

LANES = 128
P_FOX = 0
P_GLA = 1536
P_GDN = 3072
P_GDN_G = 4608
P_POOL = 5120
P_WIDTH = 5632
SM_GLA_LR = 0
SM_GDN_B = 16
SM_GDN_A = 20
SM_FOX_F = 28

VMEM_LIMIT = 56 * 1024 * 1024
BIG_VMEM_LIMIT = 60 * 1024 * 1024

MOD_TN = 1024
PREP_TK = 128
CAST_ROWS = 256
INPROJ_TM, INPROJ_TN = 2048, 512
MIXER_TB = 512
POOL_TB = 1024
FOX_CUM_BLOCK = 512
FOX_BLOCK = 512
OUTPROJ_TM, OUTPROJ_TN = 1024, 1024
FFN_TM, FFN_TF = 1024, 512


def _cparams(n_axes, vmem_limit=VMEM_LIMIT):
    return pltpu.CompilerParams(dimension_semantics=("arbitrary",) * n_axes,
                                vmem_limit_bytes=vmem_limit)


def _dot(a, b, precision=None):
    return jnp.dot(a, b, preferred_element_type=F32, precision=precision)


def _dot_nt(a, b):
    return lax.dot_general(a, b, (((1,), (1,)), ((), ())), preferred_element_type=F32)


def _bdot(a, b):
    return lax.dot_general(a, b, (((2,), (1,)), ((0,), (0,))), preferred_element_type=F32)


def _bdot_nt(a, b):
    return lax.dot_general(a, b, (((2,), (2,)), ((0,), (0,))), preferred_element_type=F32)


def _dot_sel(sel_bf16, x):
    x1 = x.astype(BF16)
    r1 = x - x1.astype(F32)
    x2 = r1.astype(BF16)
    x3 = (r1 - x2.astype(F32)).astype(BF16)
    return _dot(sel_bf16, x1) + _dot(sel_bf16, x2) + _dot(sel_bf16, x3)


def _bdot_sel(sel_bf16, x):
    x1 = x.astype(BF16)
    r1 = x - x1.astype(F32)
    x2 = r1.astype(BF16)
    x3 = (r1 - x2.astype(F32)).astype(BF16)
    return _bdot(sel_bf16, x1) + _bdot(sel_bf16, x2) + _bdot(sel_bf16, x3)


def _log_sigmoid(x):
    return jnp.minimum(x, 0.0) - jnp.log1p(jnp.exp(-jnp.abs(x)))


def _softplus(x):
    return jnp.maximum(x, 0.0) + jnp.log1p(jnp.exp(-jnp.abs(x)))


def _silu(x):
    return x * jax.nn.sigmoid(x)


def _rms_mod(x, norm_w, shift, scale):
    gain = norm_w * (1.0 + scale)
    return x * lax.rsqrt(jnp.mean(x * x, axis=-1, keepdims=True) + EPS) * gain + shift


def _cast_kernel(w_ref, o_ref):
    o_ref[...] = w_ref[...].astype(BF16)


def _to_bf16(w):
    L, R, C = w.shape
    rows = CAST_ROWS
    spec = pl.BlockSpec((1, rows, C), lambda l, r: (l, r, 0))
    return pl.pallas_call(
        _cast_kernel,
        grid=(L, R // rows),
        in_specs=[spec],
        out_specs=spec,
        out_shape=jax.ShapeDtypeStruct(w.shape, BF16),
        compiler_params=_cparams(2),
        name="weight_to_bf16",
    )(w)


def _mod_kernel(c_ref, w_ref, b_ref, o_ref):
    cond = _silu(c_ref[...])
    c_hi = cond.astype(BF16)
    c_lo = (cond - c_hi.astype(F32)).astype(BF16)
    w = w_ref[0]
    w_hi = w.astype(BF16)
    w_lo = (w - w_hi.astype(F32)).astype(BF16)
    both = _dot(jnp.concatenate([c_hi, c_lo], axis=0), w_hi)
    o_ref[0] = both[:8] + both[8:] + _dot(c_hi, w_lo) + b_ref[0]


def _modulation(c_pad, w_mod, b_mod):
    L, D, N = w_mod.shape
    tn = MOD_TN
    return pl.pallas_call(
        _mod_kernel,
        grid=(L, N // tn),
        in_specs=[pl.BlockSpec((8, D), lambda l, j: (0, 0)),
                  pl.BlockSpec((1, D, tn), lambda l, j: (l, 0, j)),
                  pl.BlockSpec((1, 1, tn), lambda l, j: (l, 0, j))],
        out_specs=pl.BlockSpec((1, 8, tn), lambda l, j: (l, 0, j)),
        out_shape=jax.ShapeDtypeStruct((L, 8, N), F32),
        compiler_params=_cparams(2),
        name="modulation",
    )(c_pad, w_mod, b_mod.reshape(L, 1, N))


def _wprep_kernel(w_ref, main_ref, small_ref):
    tk = w_ref.shape[2]
    nf = IN_SPLITS[13]
    assert (SM_GLA_LR, SM_GDN_B, SM_GDN_A, SM_FOX_F) == (0, 16, 20, 28) and nf == 4
    for l in range(w_ref.shape[1]):
        off = 0
        for i in (10, 11, 12, 0, 1, 2, 3, 5, 6, 9):
            width = IN_SPLITS[i]
            main_ref[l, :, off:off + width] = jnp.transpose(w_ref[IN_OFFS[i]:IN_OFFS[i + 1], l, :]).astype(BF16)
            off += width
        tail = w_ref[IN_OFFS[14] - 8:IN_OFFS[14], l, :]
        tail = jnp.where(lax.broadcasted_iota(jnp.int32, (8, tk), 0) >= 8 - nf, tail, 0.0)
        narrow = jnp.concatenate(
            [w_ref[IN_OFFS[4]:IN_OFFS[5], l, :],
             w_ref[IN_OFFS[7]:IN_OFFS[9], l, :],
             tail,
             jnp.zeros((LANES - 32, tk), F32)], axis=0)
        small_ref[l] = jnp.transpose(narrow).astype(BF16)


def _prepare_in_weights(w_in):
    L, D, N = w_in.shape
    tk = PREP_TK
    wt = jnp.transpose(w_in, (2, 0, 1))
    return pl.pallas_call(
        _wprep_kernel,
        grid=(D // tk,),
        in_specs=[pl.BlockSpec((N, L, tk), lambda r: (0, 0, r))],
        out_specs=[pl.BlockSpec((L, tk, P_WIDTH), lambda r: (0, r, 0)),
                   pl.BlockSpec((L, tk, LANES), lambda r: (0, r, 0))],
        out_shape=[jax.ShapeDtypeStruct((L, D, P_WIDTH), BF16),
                   jax.ShapeDtypeStruct((L, D, LANES), BF16)],
        compiler_params=_cparams(1),
        name="in_weight_prep",
    )(wt)


INPROJ_SLABS = 8


def _inproj_kernel(x_ref, mod_ref, nw_ref, w_ref, ws_ref, o_ref, os_ref, ha_ref, hb_ref):
    i = pl.program_id(0)
    j = pl.program_id(1)
    slab = x_ref.shape[0]

    def row(fill_ref, use_ref):
        def normalise_slab():
            row0 = pl.multiple_of(jnp.minimum(j, INPROJ_SLABS - 1) * slab, slab)
            h = _rms_mod(x_ref[...], nw_ref[...], mod_ref[0, 0:1, :], mod_ref[0, 1:2, :])
            fill_ref[pl.ds(row0, slab), :] = h.astype(BF16)

        @pl.when(i == 0)
        def _():
            normalise_slab()

        @pl.when(i > 0)
        def _():
            @pl.when(j == 0)
            def _():
                os_ref[...] = _dot(use_ref[...], ws_ref[0])

            normalise_slab()
            o_ref[...] = _dot(use_ref[...], w_ref[0])

    @pl.when(i % 2 == 0)
    def _():
        row(ha_ref, hb_ref)

    @pl.when(i % 2 == 1)
    def _():
        row(hb_ref, ha_ref)


def _in_projection(x2, mod, norm_w, w_main, w_small, l, T):
    M, D = x2.shape
    N = w_main.shape[2]
    tm, tn = INPROJ_TM, INPROJ_TN
    tpb = T // tm
    n_tiles = M // tm
    slab = tm // INPROJ_SLABS
    assert N // tn >= INPROJ_SLABS
    nxt = lambda i: jnp.minimum(i, n_tiles - 1)
    cur = lambda i: jnp.maximum(i - 1, 0)
    return pl.pallas_call(
        _inproj_kernel,
        grid=(n_tiles + 1, N // tn),
        in_specs=[pl.BlockSpec((slab, D), lambda i, j: (nxt(i) * INPROJ_SLABS + jnp.minimum(j, INPROJ_SLABS - 1), 0)),
                  pl.BlockSpec((1, 6, D), lambda i, j: (nxt(i) // tpb, 0, 0)),
                  pl.BlockSpec((1, D), lambda i, j: (0, 0)),
                  pl.BlockSpec((1, D, tn), lambda i, j: (l, 0, j)),
                  pl.BlockSpec((1, D, LANES), lambda i, j: (l, 0, 0))],
        out_specs=[pl.BlockSpec((tm, tn), lambda i, j: (cur(i), jnp.where(i > 0, j, 0))),
                   pl.BlockSpec((tm, LANES), lambda i, j: (cur(i), 0))],
        out_shape=[jax.ShapeDtypeStruct((M, N), F32),
                   jax.ShapeDtypeStruct((M, LANES), F32)],
        scratch_shapes=[pltpu.VMEM((tm, D), BF16), pltpu.VMEM((tm, D), BF16)],
        compiler_params=_cparams(2),
        name="in_projection",
    )(x2, mod, norm_w.reshape(1, D), w_main, w_small)


GLA_LEVELS = 7
GLA_GROUP = 8
GDN_GROUP = 8
GDN_PAIR = 2
GDN_LEVELS = 6


def _chunk_constants():
    i = np.arange(CHUNK)[:, None]
    j = np.arange(CHUNK)[None, :]
    tri = (i >= j).astype(np.float32)
    sels, masks = [], [(i == j).astype(np.float32)]
    for lv in range(1, GLA_LEVELS):
        m = 1 << (lv - 1)
        ref_row = (np.arange(CHUNK) // (2 * m)) * 2 * m + m
        sels.append((j == ref_row[:, None]).astype(np.float32))
        same = (i // (2 * m)) == (j // (2 * m))
        masks.append((same & ((i % (2 * m)) >= m) & ((j % (2 * m)) < m)).astype(np.float32))
    sel_all = np.concatenate(sels, axis=0)
    return tri, sel_all, np.stack(masks)


def _gla_kernel(p_ref, sm_ref, wlr_ref, blr_ref, nw_ref, tri_ref, sel_ref, msk_ref, bd_ref, o_ref, s_ref):
    tb = p_ref.shape[0]
    dk, dv = GLA_KEY_DIM, HEAD_DIM
    kw = N_HEADS * dk
    G = GLA_GROUP
    R = G * CHUNK
    NP = N_HEADS // 2
    NB = NP * G
    pw = 2 * dk
    vw = 2 * dv

    @pl.when(pl.program_id(1) == 0)
    def _():
        s_ref[...] = jnp.zeros_like(s_ref)

    def to_problems(a, w):
        return jnp.concatenate([a[:, p * w:(p + 1) * w].reshape(G, CHUNK, w) for p in range(NP)], axis=0)

    def pair_block_diag(a, w):
        lane = lax.broadcasted_iota(jnp.int32, a.shape, 2)
        zero = jnp.zeros_like(a)
        return jnp.concatenate([jnp.where(lane < w, a, zero), jnp.where(lane >= w, a, zero)], axis=1)

    def group(i, carry):
        r0 = pl.multiple_of(i * R, R)
        rows = pl.ds(r0, R)
        lr = sm_ref[rows, SM_GLA_LR:SM_GLA_LR + GLA_GATE_RANK]
        logits = _dot(lr, wlr_ref[...], precision=HIGHEST) + blr_ref[...]
        logg = (_log_sigmoid(logits) * (1.0 / GLA_GATE_TAU)).reshape(G, CHUNK, kw)
        tri = jnp.broadcast_to(tri_ref[...][None], (G, CHUNK, CHUNK))
        b3 = _bdot_sel(tri, logg)
        sel = jnp.broadcast_to(sel_ref[...][None], (G,) + sel_ref.shape)
        refs = _bdot_sel(sel, b3)
        b = b3.reshape(R, kw)
        q = p_ref[rows, 0:kw] * dk ** -0.5
        k = p_ref[rows, kw:2 * kw]
        att = None
        for lv in range(GLA_LEVELS):
            if lv == 0:
                qt, kt = q, k
            else:
                r = refs[:, (lv - 1) * CHUNK:lv * CHUNK, :].reshape(R, kw)
                qt = q * jnp.exp(jnp.minimum(b - r, 0.0))
                kt = k * jnp.exp(jnp.minimum(r - b, 0.0))
            qp = to_problems(qt.astype(BF16), pw)
            kp = pair_block_diag(to_problems(kt.astype(BF16), pw), dk)
            a = _bdot_nt(qp, kp) * msk_ref[lv]
            att = a if att is None else att + a
        v = p_ref[rows, 2 * kw:2 * kw + N_HEADS * dv]
        vp = to_problems(v.astype(BF16), vw)
        o = _bdot(att.astype(BF16), pair_block_diag(vp, dv))

        blast = jnp.broadcast_to(b3[:, CHUNK - 1:CHUNK, :], (G, CHUNK, kw)).reshape(R, kw)
        qe = to_problems((q * jnp.exp(b)).astype(BF16), pw)
        kl = to_problems(k * jnp.exp(blast - b), pw)
        dcol = to_problems(jnp.exp(blast), pw)
        bd = bd_ref[...]
        for p in range(NP):
            st = s_ref[p]
            for g in range(G):
                n = p * G + g
                o_n = o[n] + _dot(qe[n], st.astype(BF16))
                upd = _dot(jnp.transpose(kl[n]).astype(BF16), vp[n]) * bd
                st = st * jnp.transpose(dcol[n])[:, 0:1] + upd
                for j in range(2):
                    h = 2 * p + j
                    oh = o_n[:, j * dv:(j + 1) * dv]
                    on = oh * lax.rsqrt(jnp.mean(oh * oh, axis=-1, keepdims=True) + EPS) * nw_ref[...]
                    gate = p_ref[pl.ds(r0 + g * CHUNK, CHUNK), 2 * kw + (N_HEADS + h) * dv:2 * kw + (N_HEADS + h + 1) * dv]
                    o_ref[pl.ds(r0 + g * CHUNK, CHUNK), h * dv:(h + 1) * dv] = (on * _silu(gate)).astype(BF16)
            s_ref[p] = st
        return carry

    lax.fori_loop(0, tb // R, group, 0)


def _gla_mixer(proj, small, w_lr, b_lr, norm_w, consts, B, T):
    tri, sel_all, masks = consts
    M = proj.shape[0]
    tb = MIXER_TB
    nt = T // tb
    kw = N_HEADS * GLA_KEY_DIM
    masks2 = np.concatenate([masks, masks], axis=2)
    bd = np.kron(np.eye(2, dtype=np.float32), np.ones((GLA_KEY_DIM, HEAD_DIM), np.float32))
    const = lambda a: pl.BlockSpec(a.shape, lambda b, t: (0,) * a.ndim)
    return pl.pallas_call(
        _gla_kernel,
        grid=(B, nt),
        in_specs=[pl.BlockSpec((tb, 1536), lambda b, t: (b * nt + t, P_GLA // 1536)),
                  pl.BlockSpec((tb, LANES), lambda b, t: (b * nt + t, 0)),
                  pl.BlockSpec((GLA_GATE_RANK, kw), lambda b, t: (0, 0)),
                  pl.BlockSpec((1, kw), lambda b, t: (0, 0)),
                  pl.BlockSpec((1, HEAD_DIM), lambda b, t: (0, 0)),
                  const(tri), const(sel_all), const(masks2), const(bd)],
        out_specs=pl.BlockSpec((tb, GROUP_WIDTH), lambda b, t: (b * nt + t, 0)),
        out_shape=jax.ShapeDtypeStruct((M, GROUP_WIDTH), BF16),
        scratch_shapes=[pltpu.VMEM((N_HEADS // 2, 2 * GLA_KEY_DIM, 2 * HEAD_DIM), F32)],
        compiler_params=_cparams(2),
        name="gla_mixer",
    )(proj, small, w_lr, b_lr.reshape(1, kw), norm_w.reshape(1, HEAD_DIM),
      jnp.asarray(tri, BF16), jnp.asarray(sel_all, BF16), jnp.asarray(masks2, F32), jnp.asarray(bd, F32))


def _gdn_kernel(x_ref, g_ref, sm_ref, cw_ref, alog_ref, dtb_ref, nw_ref, tri_ref, incl_ref, lvl_ref,
                o_ref, s_ref, halo_ref, cq_ref, ku_ref, au_ref, gl_ref):
    tb = x_ref.shape[0]
    d = HEAD_DIM
    hw = N_HEADS * d
    G = GDN_GROUP
    R = G * CHUNK
    P = GDN_PAIR
    BR = P * CHUNK
    NP = N_HEADS // P

    @pl.when(pl.program_id(1) == 0)
    def _():
        s_ref[...] = jnp.zeros_like(s_ref)
        halo_ref[...] = jnp.zeros_like(halo_ref)

    def stack(a):
        w = a.shape[1] // N_HEADS
        head = lambda h: a[:, h * w:(h + 1) * w].reshape(G, CHUNK, w)
        return jnp.concatenate([jnp.concatenate([head(p * P + j) for j in range(P)], axis=1)
                                for p in range(NP)], axis=0)

    def gate_col(a, lane, last=False):
        def head(h):
            col = a[:, lane + h:lane + h + 1].reshape(G, CHUNK, 1)
            return jnp.broadcast_to(col[:, CHUNK - 1:CHUNK, :], (G, CHUNK, 1)) if last else col
        return jnp.concatenate([jnp.concatenate([head(p * P + j) for j in range(P)], axis=1)
                                for p in range(NP)], axis=0)

    incl = incl_ref[...]
    head_of_row = lax.broadcasted_iota(jnp.int32, (BR, d), 0) // CHUNK

    def block_diag(a):
        return jnp.concatenate([jnp.where(head_of_row == j, a, 0.0) for j in range(P)], axis=2)

    def prepare(i, carry):
        r0 = pl.multiple_of(i * R, R)
        rows = pl.ds(r0, R)
        prev = x_ref[pl.ds(pl.multiple_of(jnp.maximum(r0 - 8, 0), 8), 8), :]
        prev = jnp.where(i == 0, halo_ref[...], prev)
        win = jnp.concatenate([prev, x_ref[rows, :]], axis=0)
        conv = win * cw_ref[GDN_CONV - 1:GDN_CONV, :]
        for s in range(1, GDN_CONV):
            conv = conv + pltpu.roll(win, s, axis=0) * cw_ref[GDN_CONV - 1 - s:GDN_CONV - s, :]
        qkv = _silu(conv[8:, :])

        sm = sm_ref[rows, :]
        beta = jax.nn.sigmoid(sm)
        g = -jnp.exp(alog_ref[...]) * _softplus(sm + dtb_ref[...])
        gcum = _dot_sel(tri_ref[...], g)
        for j in range(G):
            gl_ref[i * G + j] = jnp.broadcast_to(jnp.exp(gcum[(j + 1) * CHUNK - 1:(j + 1) * CHUNK, :]), (8, LANES))

        qs, ks, vs = stack(qkv[:, 0:hw]), stack(qkv[:, hw:2 * hw]), stack(qkv[:, 2 * hw:3 * hw])
        qn = qs * lax.rsqrt(jnp.sum(qs * qs, axis=-1, keepdims=True) + EPS) * d ** -0.5
        kn = ks * lax.rsqrt(jnp.sum(ks * ks, axis=-1, keepdims=True) + EPS)
        bcol = gate_col(beta, SM_GDN_B)
        gcol = gate_col(gcum, SM_GDN_A)
        gmat = jnp.broadcast_to(gcol, (NP * G, BR, BR))
        grow = jnp.stack([jnp.transpose(gmat[b]) for b in range(NP * G)])
        decay = jnp.exp(jnp.minimum(gcol - grow, 0.0)) * incl
        kb = kn * bcol
        kn16 = kn.astype(BF16)
        lmat = _bdot_nt(kb.astype(BF16), kn16) * decay
        tinv = lvl_ref[GDN_LEVELS] - lmat * lvl_ref[0]
        for lv in range(1, GDN_LEVELS):
            t16 = tinv.astype(BF16)
            tl = _bdot(t16, (lmat * lvl_ref[lv]).astype(BF16))
            tinv = tinv - _bdot(tl.astype(BF16), t16)
        rhs = jnp.concatenate([vs * bcol, kb * jnp.exp(gcol)], axis=2)
        sol = _bdot(tinv.astype(BF16), rhs.astype(BF16))
        sol16 = sol.astype(BF16)
        attn = _bdot_nt(qn.astype(BF16), kn16) * decay
        aw = _bdot(attn.astype(BF16), sol16)
        qp = (qn * jnp.exp(gcol) - aw[:, :, d:]).astype(BF16)
        kdec = kn * jnp.exp(gate_col(gcum, SM_GDN_A, last=True) - gcol)
        kd = block_diag(kdec)
        for p in range(NP):
            for g_ in range(G):
                b, c = p * G + g_, i * G + g_
                kt = jnp.transpose(kd[b]).astype(BF16)
                ku = _dot(kt, sol16[b])
                for j in range(P):
                    h = p * P + j
                    ku_ref[c, h * d:(h + 1) * d, :] = ku[j * d:(j + 1) * d, :d]
                    au_ref[c, h * CHUNK:(h + 1) * CHUNK, :] = aw[b, j * CHUNK:(j + 1) * CHUNK, :d]
                    cq_ref[c * N_HEADS + h, 0:d, :] = ku[j * d:(j + 1) * d, d:].astype(BF16)
                    cq_ref[c * N_HEADS + h, d:d + CHUNK, :] = qp[b, j * CHUNK:(j + 1) * CHUNK, :]
        return carry

    lax.fori_loop(0, tb // R, prepare, 0)
    halo_ref[...] = x_ref[tb - 8:tb, :]

    def recur(c, carry):
        rows = pl.ds(pl.multiple_of(c * CHUNK, CHUNK), CHUNK)
        gl = gl_ref[c]
        for h in range(N_HEADS):
            hs = slice(h * d, (h + 1) * d)
            st = s_ref[h]
            r = _dot(cq_ref[c * N_HEADS + h], st.astype(BF16))
            s_ref[h] = st * gl[0:1, SM_GDN_A + h:SM_GDN_A + h + 1] - r[:d] + ku_ref[c, hs, :]
            oh = r[d:] + au_ref[c, h * CHUNK:(h + 1) * CHUNK, :]
            on = oh * lax.rsqrt(jnp.mean(oh * oh, axis=-1, keepdims=True) + EPS) * nw_ref[...]
            o_ref[rows, hs] = (on * _silu(g_ref[rows, hs])).astype(BF16)
        return carry

    lax.fori_loop(0, tb // CHUNK, recur, 0)


def _gdn_mixer(proj, small, conv_w, a_log, dt_bias, norm_w, consts, B, T):
    tri, _, masks = consts
    M = proj.shape[0]
    tb = MIXER_TB
    nt = T // tb
    nc = tb // CHUNK
    hb = N_HEADS * CHUNK
    pad = lambda v: jnp.zeros((1, LANES), F32).at[0, SM_GDN_A:SM_GDN_A + N_HEADS].set(v)
    tri_g = np.kron(np.eye(GDN_GROUP, dtype=np.float32), tri)
    incl_bd = np.kron(np.eye(GDN_PAIR, dtype=np.float32), tri)
    eye_p = np.eye(GDN_PAIR, dtype=np.float32)
    lvl = np.stack([np.kron(eye_p, m) for m in masks[1:]] + [np.eye(GDN_PAIR * CHUNK, dtype=np.float32)])
    assert lvl.shape[0] == GDN_LEVELS + 1
    const = lambda a: pl.BlockSpec(a.shape, lambda b, t: (0,) * a.ndim)
    return pl.pallas_call(
        _gdn_kernel,
        grid=(B, nt),
        in_specs=[pl.BlockSpec((tb, 1536), lambda b, t: (b * nt + t, P_GDN // 1536)),
                  pl.BlockSpec((tb, GROUP_WIDTH), lambda b, t: (b * nt + t, P_GDN_G // GROUP_WIDTH)),
                  pl.BlockSpec((tb, LANES), lambda b, t: (b * nt + t, 0)),
                  pl.BlockSpec((GDN_CONV, 1536), lambda b, t: (0, 0)),
                  pl.BlockSpec((1, LANES), lambda b, t: (0, 0)),
                  pl.BlockSpec((1, LANES), lambda b, t: (0, 0)),
                  pl.BlockSpec((1, HEAD_DIM), lambda b, t: (0, 0)),
                  const(tri_g), const(incl_bd), const(lvl)],
        out_specs=pl.BlockSpec((tb, GROUP_WIDTH), lambda b, t: (b * nt + t, 0)),
        out_shape=jax.ShapeDtypeStruct((M, GROUP_WIDTH), BF16),
        scratch_shapes=[pltpu.VMEM((N_HEADS, HEAD_DIM, HEAD_DIM), F32),
                        pltpu.VMEM((8, 1536), F32),
                        pltpu.VMEM((nc * N_HEADS, HEAD_DIM + CHUNK, HEAD_DIM), BF16),
                        pltpu.VMEM((nc, N_HEADS * HEAD_DIM, HEAD_DIM), F32),
                        pltpu.VMEM((nc, hb, HEAD_DIM), F32),
                        pltpu.VMEM((nc, 8, LANES), F32)],
        compiler_params=_cparams(2),
        name="gdn_mixer",
    )(proj, proj, small, conv_w, pad(a_log), pad(dt_bias), norm_w.reshape(1, HEAD_DIM),
      jnp.asarray(tri_g, BF16), jnp.asarray(incl_bd, F32), jnp.asarray(lvl, F32))


POOL_HALO = 16


def _pool_kernel(u_ref, w_ref, sc_ref, o_ref, halo_ref):
    tb = u_ref.shape[0]
    gd = GROUP_WIDTH // len(POOL_WINDOWS)
    t = pl.program_id(1)

    @pl.when(t == 0)
    def _():
        halo_ref[...] = jnp.zeros_like(halo_ref)

    pos = (t * tb + 1 + lax.broadcasted_iota(jnp.int32, (tb, 1), 0)).astype(F32)
    for gi, win in enumerate(POOL_WINDOWS):
        cols = slice(gi * gd, (gi + 1) * gd)
        u = u_ref[:, cols]
        ssum = jnp.concatenate([halo_ref[:, cols], u], axis=0)
        shift = 1
        while shift < win:
            ssum = ssum + pltpu.roll(ssum, shift, axis=0)
            shift *= 2
        mean = ssum[POOL_HALO:, :] / jnp.minimum(pos, float(win))
        y = _dot((mean - u).astype(BF16), w_ref[gi])
        o_ref[:, cols] = (y * sc_ref[:, cols]).astype(BF16)
    halo_ref[...] = u_ref[tb - POOL_HALO:tb, :]


def _pool_mixer(proj, w_bf16, scale, B, T):
    M = proj.shape[0]
    tb = POOL_TB
    nt = T // tb
    gd = GROUP_WIDTH // len(POOL_WINDOWS)
    return pl.pallas_call(
        _pool_kernel,
        grid=(B, nt),
        in_specs=[pl.BlockSpec((tb, GROUP_WIDTH), lambda b, t: (b * nt + t, P_POOL // GROUP_WIDTH)),
                  pl.BlockSpec((len(POOL_WINDOWS), gd, gd), lambda b, t: (0, 0, 0)),
                  pl.BlockSpec((1, GROUP_WIDTH), lambda b, t: (0, 0))],
        out_specs=pl.BlockSpec((tb, GROUP_WIDTH), lambda b, t: (b * nt + t, 0)),
        out_shape=jax.ShapeDtypeStruct((M, GROUP_WIDTH), BF16),
        scratch_shapes=[pltpu.VMEM((POOL_HALO, GROUP_WIDTH), F32)],
        compiler_params=_cparams(2),
        name="pool_mixer",
    )(proj, w_bf16, scale.reshape(1, GROUP_WIDTH))


def _fox_gate_kernel(sm_ref, bias_ref, tri_ref, o_ref, carry_ref):
    @pl.when(pl.program_id(1) == 0)
    def _():
        carry_ref[...] = jnp.zeros_like(carry_ref)

    lf = _log_sigmoid(sm_ref[...] + bias_ref[...])
    cum = _dot_sel(tri_ref[...], lf) + carry_ref[...]
    carry_ref[...] = cum[FOX_CUM_BLOCK - 1:FOX_CUM_BLOCK, :]
    o_ref[...] = cum


def _fox_gate_cumsum(small, f_bias, B, T):
    tb = FOX_CUM_BLOCK
    nt = T // tb
    tri = jnp.asarray(np.tril(np.ones((tb, tb), np.float32)), BF16)
    bias = jnp.zeros((1, LANES), F32).at[0, SM_FOX_F:SM_FOX_F + N_HEADS].set(f_bias)
    return pl.pallas_call(
        _fox_gate_kernel,
        grid=(B, nt),
        in_specs=[pl.BlockSpec((tb, LANES), lambda b, t: (b * nt + t, 0)),
                  pl.BlockSpec((1, LANES), lambda b, t: (0, 0)),
                  pl.BlockSpec((tb, tb), lambda b, t: (0, 0))],
        out_specs=pl.BlockSpec((tb, LANES), lambda b, t: (b * nt + t, 0)),
        out_shape=jax.ShapeDtypeStruct((B * T, LANES), F32),
        scratch_shapes=[pltpu.VMEM((1, LANES), F32)],
        compiler_params=_cparams(2),
        name="fox_gate_cumsum",
    )(small, bias, tri)


LOG2E = math.log2(math.e)


def _fox_kernel(q_ref, k_ref, v_ref, f_ref, o_ref, kb_ref, vt_ref):
    qi = pl.program_id(1)
    blk = FOX_BLOCK
    d = HEAD_DIM
    T = k_ref.shape[0]

    @pl.when(qi == 0)
    def _():
        kb_ref[...] = k_ref[...].astype(BF16)
        for h in range(N_HEADS):
            for c in range(T // blk):
                vt_ref[h, :, c * blk:(c + 1) * blk] = jnp.transpose(
                    v_ref[c * blk:(c + 1) * blk, h * d:(h + 1) * d]).astype(BF16)

    q0 = pl.multiple_of(qi * blk, blk)
    qt = [jnp.transpose(q_ref[:, h * d:(h + 1) * d] * (d ** -0.5 * LOG2E)).astype(BF16)
          for h in range(N_HEADS)]
    fcol = lambda h, start, n: f_ref[pl.ds(start, n), SM_FOX_F + h:SM_FOX_F + h + 1]
    f0 = [fcol(h, q0, 8)[0:1, :] for h in range(N_HEADS)]

    def block(ki, carry, masked):
        k0 = pl.multiple_of(ki * blk, blk)
        scores = [_dot(kb_ref[pl.ds(k0, blk), h * d:(h + 1) * d], qt[h]) for h in range(N_HEADS)]
        stats, probs = [], []
        for h in range(N_HEADS):
            m, l, _ = carry[h]
            s = scores[h] - (fcol(h, k0, blk) - f0[h]) * LOG2E
            if masked:
                key = lax.broadcasted_iota(jnp.int32, (blk, blk), 0)
                qry = lax.broadcasted_iota(jnp.int32, (blk, blk), 1)
                s = jnp.where(key <= qry, s, -jnp.inf)
            m_new = jnp.maximum(m, jnp.max(s, axis=0, keepdims=True))
            alpha = jnp.exp2(m - m_new)
            p = jnp.exp2(s - m_new)
            stats.append((m_new, alpha, alpha * l + jnp.sum(p, axis=0, keepdims=True)))
            probs.append(p.astype(BF16))
        out = []
        for h in range(N_HEADS):
            m_new, alpha, l = stats[h]
            acc = alpha * carry[h][2] + _dot(vt_ref[h, :, pl.ds(k0, blk)], probs[h])
            out.append((m_new, l, acc))
        return tuple(out)

    init = tuple((jnp.full((1, blk), -jnp.inf, F32), jnp.zeros((1, blk), F32), jnp.zeros((d, blk), F32))
                 for _ in range(N_HEADS))
    carry = lax.fori_loop(0, qi, lambda ki, c: block(ki, c, False), init)
    carry = block(qi, carry, True)
    for h in range(N_HEADS):
        _, l, acc = carry[h]
        o_ref[:, h * d:(h + 1) * d] = jnp.transpose(acc / l).astype(BF16)


def _fox_attention(proj, fcum, B, T):
    M = proj.shape[0]
    blk = FOX_BLOCK
    nq = T // blk
    return pl.pallas_call(
        _fox_kernel,
        grid=(B, nq),
        in_specs=[pl.BlockSpec((blk, GROUP_WIDTH), lambda b, i: (b * nq + i, P_FOX // GROUP_WIDTH)),
                  pl.BlockSpec((T, GROUP_WIDTH), lambda b, i: (b, P_FOX // GROUP_WIDTH + 1)),
                  pl.BlockSpec((T, GROUP_WIDTH), lambda b, i: (b, P_FOX // GROUP_WIDTH + 2)),
                  pl.BlockSpec((T, LANES), lambda b, i: (b, 0))],
        out_specs=pl.BlockSpec((blk, GROUP_WIDTH), lambda b, i: (b * nq + i, 0)),
        out_shape=jax.ShapeDtypeStruct((M, GROUP_WIDTH), BF16),
        scratch_shapes=[pltpu.VMEM((T, GROUP_WIDTH), BF16),
                        pltpu.VMEM((N_HEADS, HEAD_DIM, T), BF16)],
        compiler_params=_cparams(2),
        name="fox_attention",
    )(proj, proj, proj, fcum)


def _outproj_kernel(ya_ref, yb_ref, yc_ref, yd_ref, w_ref, x_ref, mod_ref, o_ref):
    gw = GROUP_WIDTH
    acc = _dot(ya_ref[...], w_ref[0, 0:gw, :])
    acc = acc + _dot(yb_ref[...], w_ref[0, gw:2 * gw, :])
    acc = acc + _dot(yc_ref[...], w_ref[0, 2 * gw:3 * gw, :])
    acc = acc + _dot(yd_ref[...], w_ref[0, 3 * gw:4 * gw, :])
    o_ref[...] = x_ref[...] + mod_ref[0, 2:3, :] * acc


def _out_projection(ys, w_bf16, x2, mod, l, T):
    M, D = x2.shape
    tm, tn = OUTPROJ_TM, OUTPROJ_TN
    tpb = T // tm
    yspec = pl.BlockSpec((tm, GROUP_WIDTH), lambda i, j: (i, 0))
    return pl.pallas_call(
        _outproj_kernel,
        grid=(M // tm, D // tn),
        in_specs=[yspec, yspec, yspec, yspec,
                  pl.BlockSpec((1, D, tn), lambda i, j: (l, 0, j)),
                  pl.BlockSpec((tm, tn), lambda i, j: (i, j)),
                  pl.BlockSpec((1, 6, tn), lambda i, j: (i // tpb, 0, j))],
        out_specs=pl.BlockSpec((tm, tn), lambda i, j: (i, j)),
        out_shape=jax.ShapeDtypeStruct((M, D), F32),
        compiler_params=_cparams(2),
        name="out_projection",
    )(*ys, w_bf16, x2, mod)


def _ffn_kernel(x_ref, mod_ref, nw_ref, wg_ref, wu_ref, cw_ref, cb_ref, wd_ref, fw_ref, o_ref,
                h_ref, halo_ref, *, tiles_per_seq, final_norm):
    i = pl.program_id(0)
    j = pl.program_id(1)
    tm, tf = x_ref.shape[0], wg_ref.shape[2]

    @pl.when(j == 0)
    def _():
        x = x_ref[...]
        h = _rms_mod(x, nw_ref[...], mod_ref[0, 3:4, :], mod_ref[0, 4:5, :])
        h_ref[...] = h.astype(BF16)
        o_ref[...] = x

    h = h_ref[...]
    gate = _dot(h, wg_ref[0])
    up = _dot(h, wu_ref[0])
    prev = jnp.where(i % tiles_per_seq == 0, 0.0, halo_ref[j])
    row = lax.broadcasted_iota(jnp.int32, (tm, tf), 0)
    g1 = jnp.where(row == 0, prev[7:8, :], pltpu.roll(gate, 1, axis=0))
    g2 = jnp.where(row == 0, prev[6:7, :], jnp.where(row == 1, prev[7:8, :], pltpu.roll(gate, 2, axis=0)))
    halo_ref[j] = gate[tm - 8:tm, :]
    conv = cw_ref[0, 0:1, :] * g2 + cw_ref[0, 1:2, :] * g1 + cw_ref[0, 2:3, :] * gate + cb_ref[0]
    act = (_silu(conv) * up).astype(BF16)
    o_ref[...] += mod_ref[0, 5:6, :] * _dot(act, wd_ref[0])

    if final_norm:
        @pl.when(j == pl.num_programs(1) - 1)
        def _():
            y = o_ref[...]
            o_ref[...] = y * lax.rsqrt(jnp.mean(y * y, axis=-1, keepdims=True) + EPS) * fw_ref[...]


def _conv_ffn(x2, mod, norm_w, wg, wu, conv_w, conv_b, wd, final_w, l, T):
    M, D = x2.shape
    L, _, F = wg.shape
    tm, tf = FFN_TM, FFN_TF
    tpb = T // tm
    return pl.pallas_call(
        functools.partial(_ffn_kernel, tiles_per_seq=tpb, final_norm=(l == L - 1)),
        grid=(M // tm, F // tf),
        in_specs=[pl.BlockSpec((tm, D), lambda i, j: (i, 0), pipeline_mode=pl.Buffered(1)),
                  pl.BlockSpec((1, 6, D), lambda i, j: (i // tpb, 0, 0)),
                  pl.BlockSpec((1, D), lambda i, j: (0, 0)),
                  pl.BlockSpec((1, D, tf), lambda i, j: (l, 0, j)),
                  pl.BlockSpec((1, D, tf), lambda i, j: (l, 0, j)),
                  pl.BlockSpec((1, FFN_CONV, tf), lambda i, j: (l, 0, j)),
                  pl.BlockSpec((1, 1, tf), lambda i, j: (l, 0, j)),
                  pl.BlockSpec((1, tf, D), lambda i, j: (l, j, 0)),
                  pl.BlockSpec((1, D), lambda i, j: (0, 0))],
        out_specs=pl.BlockSpec((tm, D), lambda i, j: (i, 0)),
        out_shape=jax.ShapeDtypeStruct((M, D), F32),
        scratch_shapes=[pltpu.VMEM((tm, D), BF16),
                        pltpu.VMEM((F // tf, 8, tf), F32)],
        compiler_params=_cparams(2, BIG_VMEM_LIMIT),
        name="conv_ffn",
    )(x2, mod, norm_w.reshape(1, D), wg, wu, conv_w, conv_b.reshape(L, 1, F), wd, final_w.reshape(1, D))


def kernel(x, c, w_mod, b_mod, norm_mix, norm_ffn, w_in, gla_w_lr, gla_b_lr, gla_norm, gdn_conv, gdn_a_log, gdn_dt_bias, gdn_norm, pool_w, pool_scale, fox_f_bias, w_out, ffn_w_gate, ffn_w_up, ffn_conv_w, ffn_conv_b, ffn_w_down, norm_final):
    B, T, D = x.shape
    L = w_mod.shape[0]
    assert D == D_MODEL and T % 2048 == 0 and B <= 8
    consts = _chunk_constants()

    c_pad = jnp.zeros((8, D), F32).at[:B].set(c)
    mod_all = _modulation(c_pad, w_mod, b_mod)
    w_main, w_small = _prepare_in_weights(w_in)
    w_gate, w_up, w_down, w_o = (_to_bf16(w) for w in (ffn_w_gate, ffn_w_up, ffn_w_down, w_out))
    x2 = x.reshape(B * T, D)
    for l in range(L):
        mod = mod_all[l, :B].reshape(B, 6, D)
        proj, small = _in_projection(x2, mod, norm_mix[l], w_main, w_small, l, T)
        y_a = _gla_mixer(proj, small, gla_w_lr[l], gla_b_lr[l], gla_norm[l], consts, B, T)
        y_b = _gdn_mixer(proj, small, gdn_conv[l], gdn_a_log[l], gdn_dt_bias[l], gdn_norm[l], consts, B, T)
        y_c = _pool_mixer(proj, pool_w[l].astype(BF16), pool_scale[l], B, T)
        y_d = _fox_attention(proj, _fox_gate_cumsum(small, fox_f_bias[l], B, T), B, T)
        x2 = _out_projection((y_a, y_b, y_c, y_d), w_o, x2, mod, l, T)
        x2 = _conv_ffn(x2, mod, norm_ffn[l], w_gate, w_up, ffn_conv_w, ffn_conv_b, w_down, norm_final, l, T)
    return x2.reshape(B, T, D)
```

```python
import functools
import math

import numpy as np
import jax
import jax.numpy as jnp
from jax import lax
from jax.experimental import pallas as pl
from jax.experimental.pallas import tpu as pltpu

F32 = jnp.float32
BF16 = jnp.bfloat16
HIGHEST = lax.Precision.HIGHEST

D_MODEL = 2048
GROUP_WIDTH = D_MODEL // 4
N_HEADS = 4
HEAD_DIM = GROUP_WIDTH // 4
GLA_KEY_DIM = HEAD_DIM // 2
GLA_GATE_RANK = 16
GLA_GATE_TAU = 16.0
GDN_CONV = 4
POOL_WINDOWS = (2, 4, 8, 16)
CHUNK = 64
D_FF = 256 * int(math.ceil(8 * D_MODEL / 3 / 256))
FFN_CONV = 3
EPS = 1e-6
LOG2E = math.log2(math.e)

IN_SPLITS = (256, 256, 512, 512, 16, 1536, 512, 4, 4, 512, 512, 512, 512, 4)
IN_OFFS = tuple(int(sum(IN_SPLITS[:i])) for i in range(len(IN_SPLITS) + 1))

LANES = 128
P_FOX = 0
P_GLA = 1536
P_GDN = 3072
P_GDN_G = 4608
P_POOL = 5120
P_WIDTH = 5632
SM_GLA_LR = 0
SM_GDN_B = 16
SM_GDN_A = 20
SM_FOX_F = 28

VMEM_LIMIT = 56 * 1024 * 1024
BIG_VMEM_LIMIT = 60 * 1024 * 1024

MOD_TN = 1024
PREP_TK = 128
CAST_ROWS = 256
INPROJ_TM, INPROJ_TN = 2048, 512
MIXER_TB = 512
POOL_TB = 1024
FOX_CUM_BLOCK = 512
FOX_BLOCK = 512
OUTPROJ_TM, OUTPROJ_TN = 1024, 1024
FFN_TM, FFN_TF = 1024, 512


def _cparams(n_axes, vmem_limit=VMEM_LIMIT):
    return pltpu.CompilerParams(dimension_semantics=("arbitrary",) * n_axes,
                                vmem_limit_bytes=vmem_limit)


def _dot(a, b, precision=None):
    return jnp.dot(a, b, preferred_element_type=F32, precision=precision)


def _dot_nt(a, b):
    return lax.dot_general(a, b, (((1,), (1,)), ((), ())), preferred_element_type=F32)


def _bdot(a, b):
    return lax.dot_general(a, b, (((2,), (1,)), ((0,), (0,))), preferred_element_type=F32)


def _bdot_nt(a, b):
    return lax.dot_general(a, b, (((2,), (2,)), ((0,), (0,))), preferred_element_type=F32)


def _dot_sel(sel_bf16, x):
    x1 = x.astype(BF16)
    r1 = x - x1.astype(F32)
    x2 = r1.astype(BF16)
    x3 = (r1 - x2.astype(F32)).astype(BF16)
    return _dot(sel_bf16, x1) + _dot(sel_bf16, x2) + _dot(sel_bf16, x3)


def _bdot_sel(sel_bf16, x):
    x1 = x.astype(BF16)
    r1 = x - x1.astype(F32)
    x2 = r1.astype(BF16)
    x3 = (r1 - x2.astype(F32)).astype(BF16)
    return _bdot(sel_bf16, x1) + _bdot(sel_bf16, x2) + _bdot(sel_bf16, x3)


def _log_sigmoid(x):
    return jnp.minimum(x, 0.0) - jnp.log1p(jnp.exp(-jnp.abs(x)))


def _softplus(x):
    return jnp.maximum(x, 0.0) + jnp.log1p(jnp.exp(-jnp.abs(x)))


def _silu(x):
    return x * jax.nn.sigmoid(x)


def _rms_mod(x, norm_w, shift, scale):
    gain = norm_w * (1.0 + scale)
    return x * lax.rsqrt(jnp.mean(x * x, axis=-1, keepdims=True) + EPS) * gain + shift


def _cast_kernel(w_ref, o_ref):
    o_ref[...] = w_ref[...].astype(BF16)


def _to_bf16(w):
    L, R, C = w.shape
    rows = CAST_ROWS
    spec = pl.BlockSpec((1, rows, C), lambda l, r: (l, r, 0))
    return pl.pallas_call(
        _cast_kernel,
        grid=(L, R // rows),
        in_specs=[spec],
        out_specs=spec,
        out_shape=jax.ShapeDtypeStruct(w.shape, BF16),
        compiler_params=_cparams(2),
        name="weight_to_bf16",
    )(w)


def _mod_kernel(c_ref, w_ref, b_ref, o_ref):
    cond = _silu(c_ref[...])
    c_hi = cond.astype(BF16)
    c_lo = (cond - c_hi.astype(F32)).astype(BF16)
    w = w_ref[0]
    w_hi = w.astype(BF16)
    w_lo = (w - w_hi.astype(F32)).astype(BF16)
    both = _dot(jnp.concatenate([c_hi, c_lo], axis=0), w_hi)
    o_ref[0] = both[:8] + both[8:] + _dot(c_hi, w_lo) + b_ref[0]


def _modulation(c_pad, w_mod, b_mod):
    L, D, N = w_mod.shape
    tn = MOD_TN
    return pl.pallas_call(
        _mod_kernel,
        grid=(L, N // tn),
        in_specs=[pl.BlockSpec((8, D), lambda l, j: (0, 0)),
                  pl.BlockSpec((1, D, tn), lambda l, j: (l, 0, j)),
                  pl.BlockSpec((1, 1, tn), lambda l, j: (l, 0, j))],
        out_specs=pl.BlockSpec((1, 8, tn), lambda l, j: (l, 0, j)),
        out_shape=jax.ShapeDtypeStruct((L, 8, N), F32),
        compiler_params=_cparams(2),
        name="modulation",
    )(c_pad, w_mod, b_mod.reshape(L, 1, N))


def _wprep_kernel(w_ref, main_ref, small_ref):
    tk = w_ref.shape[2]
    nf = IN_SPLITS[13]
    assert (SM_GLA_LR, SM_GDN_B, SM_GDN_A, SM_FOX_F) == (0, 16, 20, 28) and nf == 4
    for l in range(w_ref.shape[1]):
        off = 0
        for i in (10, 11, 12, 0, 1, 2, 3, 5, 6, 9):
            width = IN_SPLITS[i]
            main_ref[l, :, off:off + width] = jnp.transpose(w_ref[IN_OFFS[i]:IN_OFFS[i + 1], l, :]).astype(BF16)
            off += width
        tail = w_ref[IN_OFFS[14] - 8:IN_OFFS[14], l, :]
        tail = jnp.where(lax.broadcasted_iota(jnp.int32, (8, tk), 0) >= 8 - nf, tail, 0.0)
        narrow = jnp.concatenate(
            [w_ref[IN_OFFS[4]:IN_OFFS[5], l, :],
             w_ref[IN_OFFS[7]:IN_OFFS[9], l, :],
             tail,
             jnp.zeros((LANES - 32, tk), F32)], axis=0)
        small_ref[l] = jnp.transpose(narrow).astype(BF16)


def _prepare_in_weights(w_in):
    L, D, N = w_in.shape
    tk = PREP_TK
    wt = jnp.transpose(w_in, (2, 0, 1))
    return pl.pallas_call(
        _wprep_kernel,
        grid=(D // tk,),
        in_specs=[pl.BlockSpec((N, L, tk), lambda r: (0, 0, r))],
        out_specs=[pl.BlockSpec((L, tk, P_WIDTH), lambda r: (0, r, 0)),
                   pl.BlockSpec((L, tk, LANES), lambda r: (0, r, 0))],
        out_shape=[jax.ShapeDtypeStruct((L, D, P_WIDTH), BF16),
                   jax.ShapeDtypeStruct((L, D, LANES), BF16)],
        compiler_params=_cparams(1),
        name="in_weight_prep",
    )(wt)


INPROJ_SLABS = 8


def _inproj_kernel(x_ref, mod_ref, nw_ref, w_ref, ws_ref, o_ref, os_ref, ha_ref, hb_ref):
    i = pl.program_id(0)
    j = pl.program_id(1)
    slab = x_ref.shape[0]

    def row(fill_ref, use_ref):
        def normalise_slab():
            row0 = pl.multiple_of(jnp.minimum(j, INPROJ_SLABS - 1) * slab, slab)
            h = _rms_mod(x_ref[...], nw_ref[...], mod_ref[0, 0:1, :], mod_ref[0, 1:2, :])
            fill_ref[pl.ds(row0, slab), :] = h.astype(BF16)

        @pl.when(i == 0)
        def _():
            normalise_slab()

        @pl.when(i > 0)
        def _():
            @pl.when(j == 0)
            def _():
                os_ref[...] = _dot(use_ref[...], ws_ref[0])

            normalise_slab()
            o_ref[...] = _dot(use_ref[...], w_ref[0])

    @pl.when(i % 2 == 0)
    def _():
        row(ha_ref, hb_ref)

    @pl.when(i % 2 == 1)
    def _():
        row(hb_ref, ha_ref)


def _in_projection(x2, mod, norm_w, w_main, w_small, l, T):
    M, D = x2.shape
    N = w_main.shape[2]
    tm, tn = INPROJ_TM, INPROJ_TN
    tpb = T // tm
    n_tiles = M // tm
    slab = tm // INPROJ_SLABS
    assert N // tn >= INPROJ_SLABS
    nxt = lambda i: jnp.minimum(i, n_tiles - 1)
    cur = lambda i: jnp.maximum(i - 1, 0)
    return pl.pallas_call(
        _inproj_kernel,
        grid=(n_tiles + 1, N // tn),
        in_specs=[pl.BlockSpec((slab, D), lambda i, j: (nxt(i) * INPROJ_SLABS + jnp.minimum(j, INPROJ_SLABS - 1), 0)),
                  pl.BlockSpec((1, 6, D), lambda i, j: (nxt(i) // tpb, 0, 0)),
                  pl.BlockSpec((1, D), lambda i, j: (0, 0)),
                  pl.BlockSpec((1, D, tn), lambda i, j: (l, 0, j)),
                  pl.BlockSpec((1, D, LANES), lambda i, j: (l, 0, 0))],
        out_specs=[pl.BlockSpec((tm, tn), lambda i, j: (cur(i), jnp.where(i > 0, j, 0))),
                   pl.BlockSpec((tm, LANES), lambda i, j: (cur(i), 0))],
        out_shape=[jax.ShapeDtypeStruct((M, N), F32),
                   jax.ShapeDtypeStruct((M, LANES), F32)],
        scratch_shapes=[pltpu.VMEM((tm, D), BF16), pltpu.VMEM((tm, D), BF16)],
        compiler_params=_cparams(2),
        name="in_projection",
    )(x2, mod, norm_w.reshape(1, D), w_main, w_small)


GLA_LEVELS = 7
GLA_GROUP = 8
GDN_GROUP = 8
GDN_PAIR = 2
GDN_LEVELS = 6


def _chunk_constants():
    i = np.arange(CHUNK)[:, None]
    j = np.arange(CHUNK)[None, :]
    tri = (i >= j).astype(np.float32)
    sels, masks = [], [(i == j).astype(np.float32)]
    for lv in range(1, GLA_LEVELS):
        m = 1 << (lv - 1)
        ref_row = (np.arange(CHUNK) // (2 * m)) * 2 * m + m
        sels.append((j == ref_row[:, None]).astype(np.float32))
        same = (i // (2 * m)) == (j // (2 * m))
        masks.append((same & ((i % (2 * m)) >= m) & ((j % (2 * m)) < m)).astype(np.float32))
    sel_all = np.concatenate(sels, axis=0)
    return tri, sel_all, np.stack(masks)


def _gla_kernel(p_ref, sm_ref, wlr_ref, blr_ref, nw_ref, tri_ref, sel_ref, msk_ref, bd_ref, o_ref, s_ref):
    tb = p_ref.shape[0]
    dk, dv = GLA_KEY_DIM, HEAD_DIM
    kw = N_HEADS * dk
    G = GLA_GROUP
    R = G * CHUNK
    NP = N_HEADS // 2
    NB = NP * G
    pw = 2 * dk
    vw = 2 * dv

    @pl.when(pl.program_id(1) == 0)
    def _():
        s_ref[...] = jnp.zeros_like(s_ref)

    def to_problems(a, w):
        return jnp.concatenate([a[:, p * w:(p + 1) * w].reshape(G, CHUNK, w) for p in range(NP)], axis=0)

    def pair_block_diag(a, w):
        lane = lax.broadcasted_iota(jnp.int32, a.shape, 2)
        zero = jnp.zeros_like(a)
        return jnp.concatenate([jnp.where(lane < w, a, zero), jnp.where(lane >= w, a, zero)], axis=1)

    def group(i, carry):
        r0 = pl.multiple_of(i * R, R)
        rows = pl.ds(r0, R)
        lr = sm_ref[rows, SM_GLA_LR:SM_GLA_LR + GLA_GATE_RANK]
        logits = _dot(lr, wlr_ref[...], precision=HIGHEST) + blr_ref[...]
        logg = (_log_sigmoid(logits) * (LOG2E / GLA_GATE_TAU)).reshape(G, CHUNK, kw)
        tri = jnp.broadcast_to(tri_ref[...][None], (G, CHUNK, CHUNK))
        b3 = _bdot_sel(tri, logg)
        sel = jnp.broadcast_to(sel_ref[...][None], (G,) + sel_ref.shape)
        refs = _bdot_sel(sel, b3)
        b = b3.reshape(R, kw)
        q = p_ref[rows, 0:kw] * dk ** -0.5
        k = p_ref[rows, kw:2 * kw]
        att = None
        for lv in range(GLA_LEVELS):
            if lv == 0:
                qt, kt = q, k
            else:
                r = refs[:, (lv - 1) * CHUNK:lv * CHUNK, :].reshape(R, kw)
                qt = q * jnp.exp2(jnp.minimum(b - r, 0.0))
                kt = k * jnp.exp2(jnp.minimum(r - b, 0.0))
            qp = to_problems(qt.astype(BF16), pw)
            kp = pair_block_diag(to_problems(kt.astype(BF16), pw), dk)
            a = _bdot_nt(qp, kp) * msk_ref[lv]
            att = a if att is None else att + a
        v = p_ref[rows, 2 * kw:2 * kw + N_HEADS * dv]
        vp = to_problems(v.astype(BF16), vw)
        o = _bdot(att.astype(BF16), pair_block_diag(vp, dv))

        blast = jnp.broadcast_to(b3[:, CHUNK - 1:CHUNK, :], (G, CHUNK, kw)).reshape(R, kw)
        qe = to_problems((q * jnp.exp2(b)).astype(BF16), pw)
        kl = to_problems(k * jnp.exp2(blast - b), pw)
        dcol = to_problems(jnp.exp2(blast), pw)
        klt = jnp.stack([jnp.transpose(kl[n]) for n in range(NB)]).astype(BF16)
        upd = _bdot(klt, vp) * bd_ref[...]
        for p in range(NP):
            st = s_ref[p]
            for g in range(G):
                n = p * G + g
                o_n = o[n] + _dot(qe[n], st.astype(BF16))
                st = st * jnp.transpose(dcol[n])[:, 0:1] + upd[n]
                for j in range(2):
                    h = 2 * p + j
                    oh = o_n[:, j * dv:(j + 1) * dv]
                    on = oh * lax.rsqrt(jnp.mean(oh * oh, axis=-1, keepdims=True) + EPS) * nw_ref[...]
                    gate = p_ref[pl.ds(r0 + g * CHUNK, CHUNK), 2 * kw + (N_HEADS + h) * dv:2 * kw + (N_HEADS + h + 1) * dv]
                    o_ref[pl.ds(r0 + g * CHUNK, CHUNK), h * dv:(h + 1) * dv] = (on * _silu(gate)).astype(BF16)
            s_ref[p] = st
        return carry

    lax.fori_loop(0, tb // R, group, 0)


def _gla_mixer(proj, small, w_lr, b_lr, norm_w, consts, B, T):
    tri, sel_all, masks = consts
    M = proj.shape[0]
    tb = MIXER_TB
    nt = T // tb
    kw = N_HEADS * GLA_KEY_DIM
    masks2 = np.concatenate([masks, masks], axis=2)
    bd = np.kron(np.eye(2, dtype=np.float32), np.ones((GLA_KEY_DIM, HEAD_DIM), np.float32))
    const = lambda a: pl.BlockSpec(a.shape, lambda b, t: (0,) * a.ndim)
    return pl.pallas_call(
        _gla_kernel,
        grid=(B, nt),
        in_specs=[pl.BlockSpec((tb, 1536), lambda b, t: (b * nt + t, P_GLA // 1536)),
                  pl.BlockSpec((tb, LANES), lambda b, t: (b * nt + t, 0)),
                  pl.BlockSpec((GLA_GATE_RANK, kw), lambda b, t: (0, 0)),
                  pl.BlockSpec((1, kw), lambda b, t: (0, 0)),
                  pl.BlockSpec((1, HEAD_DIM), lambda b, t: (0, 0)),
                  const(tri), const(sel_all), const(masks2), const(bd)],
        out_specs=pl.BlockSpec((tb, GROUP_WIDTH), lambda b, t: (b * nt + t, 0)),
        out_shape=jax.ShapeDtypeStruct((M, GROUP_WIDTH), BF16),
        scratch_shapes=[pltpu.VMEM((N_HEADS // 2, 2 * GLA_KEY_DIM, 2 * HEAD_DIM), F32)],
        compiler_params=_cparams(2),
        name="gla_mixer",
    )(proj, small, w_lr, b_lr.reshape(1, kw), norm_w.reshape(1, HEAD_DIM),
      jnp.asarray(tri, BF16), jnp.asarray(sel_all, BF16), jnp.asarray(masks2, F32), jnp.asarray(bd, F32))


def _gdn_kernel(x_ref, g_ref, sm_ref, cw_ref, alog_ref, dtb_ref, nw_ref, tri_ref, incl_ref, lvl_ref,
                o_ref, s_ref, halo_ref, cq_ref, ku_ref, au_ref, gl_ref):
    tb = x_ref.shape[0]
    d = HEAD_DIM
    hw = N_HEADS * d
    G = GDN_GROUP
    R = G * CHUNK
    P = GDN_PAIR
    BR = P * CHUNK
    NP = N_HEADS // P

    @pl.when(pl.program_id(1) == 0)
    def _():
        s_ref[...] = jnp.zeros_like(s_ref)
        halo_ref[...] = jnp.zeros_like(halo_ref)

    def stack(a):
        w = a.shape[1] // N_HEADS
        head = lambda h: a[:, h * w:(h + 1) * w].reshape(G, CHUNK, w)
        return jnp.concatenate([jnp.concatenate([head(p * P + j) for j in range(P)], axis=1)
                                for p in range(NP)], axis=0)

    def gate_col(a, lane, last=False):
        def head(h):
            col = a[:, lane + h:lane + h + 1].reshape(G, CHUNK, 1)
            return jnp.broadcast_to(col[:, CHUNK - 1:CHUNK, :], (G, CHUNK, 1)) if last else col
        return jnp.concatenate([jnp.concatenate([head(p * P + j) for j in range(P)], axis=1)
                                for p in range(NP)], axis=0)

    incl = incl_ref[...]
    head_of_row = lax.broadcasted_iota(jnp.int32, (BR, d), 0) // CHUNK

    def block_diag(a):
        return jnp.concatenate([jnp.where(head_of_row == j, a, 0.0) for j in range(P)], axis=2)

    def prepare(i, carry):
        r0 = pl.multiple_of(i * R, R)
        rows = pl.ds(r0, R)
        prev = x_ref[pl.ds(pl.multiple_of(jnp.maximum(r0 - 8, 0), 8), 8), :]
        prev = jnp.where(i == 0, halo_ref[...], prev)
        win = jnp.concatenate([prev, x_ref[rows, :]], axis=0)
        conv = win * cw_ref[GDN_CONV - 1:GDN_CONV, :]
        for s in range(1, GDN_CONV):
            conv = conv + pltpu.roll(win, s, axis=0) * cw_ref[GDN_CONV - 1 - s:GDN_CONV - s, :]
        qkv = _silu(conv[8:, :])

        sm = sm_ref[rows, :]
        beta = jax.nn.sigmoid(sm)
        g = -(jnp.exp(alog_ref[...]) * LOG2E) * _softplus(sm + dtb_ref[...])
        gcum = _dot_sel(tri_ref[...], g)
        for j in range(G):
            gl_ref[i * G + j] = jnp.broadcast_to(jnp.exp2(gcum[(j + 1) * CHUNK - 1:(j + 1) * CHUNK, :]), (8, LANES))

        qs, ks, vs = stack(qkv[:, 0:hw]), stack(qkv[:, hw:2 * hw]), stack(qkv[:, 2 * hw:3 * hw])
        qn = qs * lax.rsqrt(jnp.sum(qs * qs, axis=-1, keepdims=True) + EPS) * d ** -0.5
        kn = ks * lax.rsqrt(jnp.sum(ks * ks, axis=-1, keepdims=True) + EPS)
        bcol = gate_col(beta, SM_GDN_B)
        gcol = gate_col(gcum, SM_GDN_A)
        gmat = jnp.broadcast_to(gcol, (NP * G, BR, BR))
        grow = jnp.stack([jnp.transpose(gmat[b]) for b in range(NP * G)])
        decay = jnp.exp2(jnp.minimum(gcol - grow, 0.0)) * incl
        kb = kn * bcol
        kn16 = kn.astype(BF16)
        lmat = _bdot_nt(kb.astype(BF16), kn16) * decay
        tinv = lvl_ref[GDN_LEVELS] - lmat * lvl_ref[0]
        for lv in range(1, GDN_LEVELS):
            t16 = tinv.astype(BF16)
            tl = _bdot(t16, (lmat * lvl_ref[lv]).astype(BF16))
            tinv = tinv - _bdot(tl.astype(BF16), t16)
        rhs = jnp.concatenate([vs * bcol, kb * jnp.exp2(gcol)], axis=2)
        sol = _bdot(tinv.astype(BF16), rhs.astype(BF16))
        sol16 = sol.astype(BF16)
        attn = _bdot_nt(qn.astype(BF16), kn16) * decay
        aw = _bdot(attn.astype(BF16), sol16)
        qp = (qn * jnp.exp2(gcol) - aw[:, :, d:]).astype(BF16)
        kdec = kn * jnp.exp2(gate_col(gcum, SM_GDN_A, last=True) - gcol)
        kd = block_diag(kdec)
        for p in range(NP):
            for g_ in range(G):
                b, c = p * G + g_, i * G + g_
                kt = jnp.transpose(kd[b]).astype(BF16)
                ku = _dot(kt, sol16[b])
                for j in range(P):
                    h = p * P + j
                    ku_ref[c, h * d:(h + 1) * d, :] = ku[j * d:(j + 1) * d, :d]
                    au_ref[c, h * CHUNK:(h + 1) * CHUNK, :] = aw[b, j * CHUNK:(j + 1) * CHUNK, :d]
                    cq_ref[c * N_HEADS + h, 0:d, :] = ku[j * d:(j + 1) * d, d:].astype(BF16)
                    cq_ref[c * N_HEADS + h, d:d + CHUNK, :] = qp[b, j * CHUNK:(j + 1) * CHUNK, :]
        return carry

    lax.fori_loop(0, tb // R, prepare, 0)
    halo_ref[...] = x_ref[tb - 8:tb, :]

    def recur(c, carry):
        rows = pl.ds(pl.multiple_of(c * CHUNK, CHUNK), CHUNK)
        gl = gl_ref[c]
        for h in range(N_HEADS):
            hs = slice(h * d, (h + 1) * d)
            st = s_ref[h]
            r = _dot(cq_ref[c * N_HEADS + h], st.astype(BF16))
            s_ref[h] = st * gl[0:1, SM_GDN_A + h:SM_GDN_A + h + 1] - r[:d] + ku_ref[c, hs, :]
            oh = r[d:] + au_ref[c, h * CHUNK:(h + 1) * CHUNK, :]
            on = oh * lax.rsqrt(jnp.mean(oh * oh, axis=-1, keepdims=True) + EPS) * nw_ref[...]
            o_ref[rows, hs] = (on * _silu(g_ref[rows, hs])).astype(BF16)
        return carry

    lax.fori_loop(0, tb // CHUNK, recur, 0)


def _gdn_mixer(proj, small, conv_w, a_log, dt_bias, norm_w, consts, B, T):
    tri, _, masks = consts
    M = proj.shape[0]
    tb = MIXER_TB
    nt = T // tb
    nc = tb // CHUNK
    hb = N_HEADS * CHUNK
    pad = lambda v: jnp.zeros((1, LANES), F32).at[0, SM_GDN_A:SM_GDN_A + N_HEADS].set(v)
    tri_g = np.kron(np.eye(GDN_GROUP, dtype=np.float32), tri)
    incl_bd = np.kron(np.eye(GDN_PAIR, dtype=np.float32), tri)
    eye_p = np.eye(GDN_PAIR, dtype=np.float32)
    lvl = np.stack([np.kron(eye_p, m) for m in masks[1:]] + [np.eye(GDN_PAIR * CHUNK, dtype=np.float32)])
    assert lvl.shape[0] == GDN_LEVELS + 1
    const = lambda a: pl.BlockSpec(a.shape, lambda b, t: (0,) * a.ndim)
    return pl.pallas_call(
        _gdn_kernel,
        grid=(B, nt),
        in_specs=[pl.BlockSpec((tb, 1536), lambda b, t: (b * nt + t, P_GDN // 1536)),
                  pl.BlockSpec((tb, GROUP_WIDTH), lambda b, t: (b * nt + t, P_GDN_G // GROUP_WIDTH)),
                  pl.BlockSpec((tb, LANES), lambda b, t: (b * nt + t, 0)),
                  pl.BlockSpec((GDN_CONV, 1536), lambda b, t: (0, 0)),
                  pl.BlockSpec((1, LANES), lambda b, t: (0, 0)),
                  pl.BlockSpec((1, LANES), lambda b, t: (0, 0)),
                  pl.BlockSpec((1, HEAD_DIM), lambda b, t: (0, 0)),
                  const(tri_g), const(incl_bd), const(lvl)],
        out_specs=pl.BlockSpec((tb, GROUP_WIDTH), lambda b, t: (b * nt + t, 0)),
        out_shape=jax.ShapeDtypeStruct((M, GROUP_WIDTH), BF16),
        scratch_shapes=[pltpu.VMEM((N_HEADS, HEAD_DIM, HEAD_DIM), F32),
                        pltpu.VMEM((8, 1536), F32),
                        pltpu.VMEM((nc * N_HEADS, HEAD_DIM + CHUNK, HEAD_DIM), BF16),
                        pltpu.VMEM((nc, N_HEADS * HEAD_DIM, HEAD_DIM), F32),
                        pltpu.VMEM((nc, hb, HEAD_DIM), F32),
                        pltpu.VMEM((nc, 8, LANES), F32)],
        compiler_params=_cparams(2),
        name="gdn_mixer",
    )(proj, proj, small, conv_w, pad(a_log), pad(dt_bias), norm_w.reshape(1, HEAD_DIM),
      jnp.asarray(tri_g, BF16), jnp.asarray(incl_bd, F32), jnp.asarray(lvl, F32))


POOL_HALO = 16


def _pool_kernel(u_ref, w_ref, sc_ref, o_ref, halo_ref):
    tb = u_ref.shape[0]
    gd = GROUP_WIDTH // len(POOL_WINDOWS)
    t = pl.program_id(1)

    @pl.when(t == 0)
    def _():
        halo_ref[...] = jnp.zeros_like(halo_ref)

    pos = (t * tb + 1 + lax.broadcasted_iota(jnp.int32, (tb, 1), 0)).astype(F32)
    for gi, win in enumerate(POOL_WINDOWS):
        cols = slice(gi * gd, (gi + 1) * gd)
        u = u_ref[:, cols]
        ssum = jnp.concatenate([halo_ref[:, cols], u], axis=0)
        shift = 1
        while shift < win:
            ssum = ssum + pltpu.roll(ssum, shift, axis=0)
            shift *= 2
        mean = ssum[POOL_HALO:, :] / jnp.minimum(pos, float(win))
        y = _dot((mean - u).astype(BF16), w_ref[gi])
        o_ref[:, cols] = (y * sc_ref[:, cols]).astype(BF16)
    halo_ref[...] = u_ref[tb - POOL_HALO:tb, :]


def _pool_mixer(proj, w_bf16, scale, B, T):
    M = proj.shape[0]
    tb = POOL_TB
    nt = T // tb
    gd = GROUP_WIDTH // len(POOL_WINDOWS)
    return pl.pallas_call(
        _pool_kernel,
        grid=(B, nt),
        in_specs=[pl.BlockSpec((tb, GROUP_WIDTH), lambda b, t: (b * nt + t, P_POOL // GROUP_WIDTH)),
                  pl.BlockSpec((len(POOL_WINDOWS), gd, gd), lambda b, t: (0, 0, 0)),
                  pl.BlockSpec((1, GROUP_WIDTH), lambda b, t: (0, 0))],
        out_specs=pl.BlockSpec((tb, GROUP_WIDTH), lambda b, t: (b * nt + t, 0)),
        out_shape=jax.ShapeDtypeStruct((M, GROUP_WIDTH), BF16),
        scratch_shapes=[pltpu.VMEM((POOL_HALO, GROUP_WIDTH), F32)],
        compiler_params=_cparams(2),
        name="pool_mixer",
    )(proj, w_bf16, scale.reshape(1, GROUP_WIDTH))


def _fox_gate_kernel(sm_ref, bias_ref, tri_ref, o_ref, carry_ref):
    @pl.when(pl.program_id(1) == 0)
    def _():
        carry_ref[...] = jnp.zeros_like(carry_ref)

    lf = _log_sigmoid(sm_ref[...] + bias_ref[...])
    cum = _dot_sel(tri_ref[...], lf) + carry_ref[...]
    carry_ref[...] = cum[FOX_CUM_BLOCK - 1:FOX_CUM_BLOCK, :]
    o_ref[...] = cum


def _fox_gate_cumsum(small, f_bias, B, T):
    tb = FOX_CUM_BLOCK
    nt = T // tb
    tri = jnp.asarray(np.tril(np.ones((tb, tb), np.float32)), BF16)
    bias = jnp.zeros((1, LANES), F32).at[0, SM_FOX_F:SM_FOX_F + N_HEADS].set(f_bias)
    return pl.pallas_call(
        _fox_gate_kernel,
        grid=(B, nt),
        in_specs=[pl.BlockSpec((tb, LANES), lambda b, t: (b * nt + t, 0)),
                  pl.BlockSpec((1, LANES), lambda b, t: (0, 0)),
                  pl.BlockSpec((tb, tb), lambda b, t: (0, 0))],
        out_specs=pl.BlockSpec((tb, LANES), lambda b, t: (b * nt + t, 0)),
        out_shape=jax.ShapeDtypeStruct((B * T, LANES), F32),
        scratch_shapes=[pltpu.VMEM((1, LANES), F32)],
        compiler_params=_cparams(2),
        name="fox_gate_cumsum",
    )(small, bias, tri)


def _fox_kernel(q_ref, k_ref, v_ref, f_ref, o_ref, kb_ref, vt_ref):
    qi = pl.program_id(1)
    blk = FOX_BLOCK
    d = HEAD_DIM
    T = k_ref.shape[0]

    @pl.when(qi == 0)
    def _():
        kb_ref[...] = k_ref[...].astype(BF16)
        for h in range(N_HEADS):
            for c in range(T // blk):
                vt_ref[h, :, c * blk:(c + 1) * blk] = jnp.transpose(
                    v_ref[c * blk:(c + 1) * blk, h * d:(h + 1) * d]).astype(BF16)

    q0 = pl.multiple_of(qi * blk, blk)
    qt = [jnp.transpose(q_ref[:, h * d:(h + 1) * d] * (d ** -0.5 * LOG2E)).astype(BF16)
          for h in range(N_HEADS)]
    fcol = lambda h, start, n: f_ref[pl.ds(start, n), SM_FOX_F + h:SM_FOX_F + h + 1]
    f0 = [fcol(h, q0, 8)[0:1, :] for h in range(N_HEADS)]

    def block(ki, carry, masked):
        k0 = pl.multiple_of(ki * blk, blk)
        scores = [_dot(kb_ref[pl.ds(k0, blk), h * d:(h + 1) * d], qt[h]) for h in range(N_HEADS)]
        stats, probs = [], []
        for h in range(N_HEADS):
            m, l, _ = carry[h]
            s = scores[h] - (fcol(h, k0, blk) - f0[h]) * LOG2E
            if masked:
                key = lax.broadcasted_iota(jnp.int32, (blk, blk), 0)
                qry = lax.broadcasted_iota(jnp.int32, (blk, blk), 1)
                s = jnp.where(key <= qry, s, -jnp.inf)
            m_new = jnp.maximum(m, jnp.max(s, axis=0, keepdims=True))
            alpha = jnp.exp2(m - m_new)
            p = jnp.exp2(s - m_new)
            stats.append((m_new, alpha, alpha * l + jnp.sum(p, axis=0, keepdims=True)))
            probs.append(p.astype(BF16))
        out = []
        for h in range(N_HEADS):
            m_new, alpha, l = stats[h]
            acc = alpha * carry[h][2] + _dot(vt_ref[h, :, pl.ds(k0, blk)], probs[h])
            out.append((m_new, l, acc))
        return tuple(out)

    init = tuple((jnp.full((1, blk), -jnp.inf, F32), jnp.zeros((1, blk), F32), jnp.zeros((d, blk), F32))
                 for _ in range(N_HEADS))
    carry = lax.fori_loop(0, qi, lambda ki, c: block(ki, c, False), init)
    carry = block(qi, carry, True)
    for h in range(N_HEADS):
        _, l, acc = carry[h]
        o_ref[:, h * d:(h + 1) * d] = jnp.transpose(acc / l).astype(BF16)


def _fox_attention(proj, fcum, B, T):
    M = proj.shape[0]
    blk = FOX_BLOCK
    nq = T // blk
    return pl.pallas_call(
        _fox_kernel,
        grid=(B, nq),
        in_specs=[pl.BlockSpec((blk, GROUP_WIDTH), lambda b, i: (b * nq + i, P_FOX // GROUP_WIDTH)),
                  pl.BlockSpec((T, GROUP_WIDTH), lambda b, i: (b, P_FOX // GROUP_WIDTH + 1)),
                  pl.BlockSpec((T, GROUP_WIDTH), lambda b, i: (b, P_FOX // GROUP_WIDTH + 2)),
                  pl.BlockSpec((T, LANES), lambda b, i: (b, 0))],
        out_specs=pl.BlockSpec((blk, GROUP_WIDTH), lambda b, i: (b * nq + i, 0)),
        out_shape=jax.ShapeDtypeStruct((M, GROUP_WIDTH), BF16),
        scratch_shapes=[pltpu.VMEM((T, GROUP_WIDTH), BF16),
                        pltpu.VMEM((N_HEADS, HEAD_DIM, T), BF16)],
        compiler_params=_cparams(2),
        name="fox_attention",
    )(proj, proj, proj, fcum)


def _outproj_kernel(ya_ref, yb_ref, yc_ref, yd_ref, w_ref, x_ref, mod_ref, o_ref):
    gw = GROUP_WIDTH
    acc = _dot(ya_ref[...], w_ref[0, 0:gw, :])
    acc = acc + _dot(yb_ref[...], w_ref[0, gw:2 * gw, :])
    acc = acc + _dot(yc_ref[...], w_ref[0, 2 * gw:3 * gw, :])
    acc = acc + _dot(yd_ref[...], w_ref[0, 3 * gw:4 * gw, :])
    o_ref[...] = x_ref[...] + mod_ref[0, 2:3, :] * acc


def _out_projection(ys, w_bf16, x2, mod, l, T):
    M, D = x2.shape
    tm, tn = OUTPROJ_TM, OUTPROJ_TN
    tpb = T // tm
    yspec = pl.BlockSpec((tm, GROUP_WIDTH), lambda i, j: (i, 0))
    return pl.pallas_call(
        _outproj_kernel,
        grid=(M // tm, D // tn),
        in_specs=[yspec, yspec, yspec, yspec,
                  pl.BlockSpec((1, D, tn), lambda i, j: (l, 0, j)),
                  pl.BlockSpec((tm, tn), lambda i, j: (i, j)),
                  pl.BlockSpec((1, 6, tn), lambda i, j: (i // tpb, 0, j))],
        out_specs=pl.BlockSpec((tm, tn), lambda i, j: (i, j)),
        out_shape=jax.ShapeDtypeStruct((M, D), F32),
        compiler_params=_cparams(2),
        name="out_projection",
    )(*ys, w_bf16, x2, mod)


def _ffn_kernel(x_ref, mod_ref, nw_ref, wg_ref, wu_ref, cw_ref, cb_ref, wd_ref, fw_ref, o_ref,
                h_ref, halo_ref, *, tiles_per_seq, final_norm):
    i = pl.program_id(0)
    j = pl.program_id(1)
    tm, tf = x_ref.shape[0], wg_ref.shape[2]

    @pl.when(j == 0)
    def _():
        x = x_ref[...]
        h = _rms_mod(x, nw_ref[...], mod_ref[0, 3:4, :], mod_ref[0, 4:5, :])
        h_ref[...] = h.astype(BF16)
        o_ref[...] = x

    h = h_ref[...]
    gate = _dot(h, wg_ref[0])
    up = _dot(h, wu_ref[0])
    prev = jnp.where(i % tiles_per_seq == 0, 0.0, halo_ref[j])
    row = lax.broadcasted_iota(jnp.int32, (tm, tf), 0)
    g1 = jnp.where(row == 0, prev[7:8, :], pltpu.roll(gate, 1, axis=0))
    g2 = jnp.where(row == 0, prev[6:7, :], jnp.where(row == 1, prev[7:8, :], pltpu.roll(gate, 2, axis=0)))
    halo_ref[j] = gate[tm - 8:tm, :]
    conv = cw_ref[0, 0:1, :] * g2 + cw_ref[0, 1:2, :] * g1 + cw_ref[0, 2:3, :] * gate + cb_ref[0]
    act = (_silu(conv) * up).astype(BF16)
    o_ref[...] += mod_ref[0, 5:6, :] * _dot(act, wd_ref[0])

    if final_norm:
        @pl.when(j == pl.num_programs(1) - 1)
        def _():
            y = o_ref[...]
            o_ref[...] = y * lax.rsqrt(jnp.mean(y * y, axis=-1, keepdims=True) + EPS) * fw_ref[...]


def _conv_ffn(x2, mod, norm_w, wg, wu, conv_w, conv_b, wd, final_w, l, T):
    M, D = x2.shape
    L, _, F = wg.shape
    tm, tf = FFN_TM, FFN_TF
    tpb = T // tm
    return pl.pallas_call(
        functools.partial(_ffn_kernel, tiles_per_seq=tpb, final_norm=(l == L - 1)),
        grid=(M // tm, F // tf),
        in_specs=[pl.BlockSpec((tm, D), lambda i, j: (i, 0), pipeline_mode=pl.Buffered(1)),
                  pl.BlockSpec((1, 6, D), lambda i, j: (i // tpb, 0, 0)),
                  pl.BlockSpec((1, D), lambda i, j: (0, 0)),
                  pl.BlockSpec((1, D, tf), lambda i, j: (l, 0, j)),
                  pl.BlockSpec((1, D, tf), lambda i, j: (l, 0, j)),
                  pl.BlockSpec((1, FFN_CONV, tf), lambda i, j: (l, 0, j)),
                  pl.BlockSpec((1, 1, tf), lambda i, j: (l, 0, j)),
                  pl.BlockSpec((1, tf, D), lambda i, j: (l, j, 0)),
                  pl.BlockSpec((1, D), lambda i, j: (0, 0))],
        out_specs=pl.BlockSpec((tm, D), lambda i, j: (i, 0)),
        out_shape=jax.ShapeDtypeStruct((M, D), F32),
        scratch_shapes=[pltpu.VMEM((tm, D), BF16),
                        pltpu.VMEM((F // tf, 8, tf), F32)],
        compiler_params=_cparams(2, BIG_VMEM_LIMIT),
        name="conv_ffn",
    )(x2, mod, norm_w.reshape(1, D), wg, wu, conv_w, conv_b.reshape(L, 1, F), wd, final_w.reshape(1, D))


def kernel(x, c, w_mod, b_mod, norm_mix, norm_ffn, w_in, gla_w_lr, gla_b_lr, gla_norm, gdn_conv, gdn_a_log, gdn_dt_bias, gdn_norm, pool_w, pool_scale, fox_f_bias, w_out, ffn_w_gate, ffn_w_up, ffn_conv_w, ffn_conv_b, ffn_w_down, norm_final):
    B, T, D = x.shape
    L = w_mod.shape[0]
    assert D == D_MODEL and T % 2048 == 0 and B <= 8
    consts = _chunk_constants()

    c_pad = jnp.zeros((8, D), F32).at[:B].set(c)
    mod_all = _modulation(c_pad, w_mod, b_mod)
    w_main, w_small = _prepare_in_weights(w_in)
    w_gate, w_up, w_down, w_o = (_to_bf16(w) for w in (ffn_w_gate, ffn_w_up, ffn_w_down, w_out))
    x2 = x.reshape(B * T, D)
    for l in range(L):
        mod = mod_all[l, :B].reshape(B, 6, D)
        proj, small = _in_projection(x2, mod, norm_mix[l], w_main, w_small, l, T)
        y_a = _gla_mixer(proj, small, gla_w_lr[l], gla_b_lr[l], gla_norm[l], consts, B, T)
        y_b = _gdn_mixer(proj, small, gdn_conv[l], gdn_a_log[l], gdn_dt_bias[l], gdn_norm[l], consts, B, T)
        y_c = _pool_mixer(proj, pool_w[l].astype(BF16), pool_scale[l], B, T)
        y_d = _fox_attention(proj, _fox_gate_cumsum(small, fox_f_bias[l], B, T), B, T)
        x2 = _out_projection((y_a, y_b, y_c, y_d), w_o, x2, mod, l, T)
        x2 = _conv_ffn(x2, mod, norm_ffn[l], w_gate, w_up, ffn_conv_w, ffn_conv_b, w_down, norm_final, l, T)
    return x2.reshape(B, T, D)
```
